```python
import math
import jax, jax.numpy as jnp
from jax import lax
import numpy as np

D_MODEL = 1024
BATCH = 2
SEQ = 8192
DEPTH = 2
DEC_BATCH = 32
DEC_SEQ = 4
PAST_LEN = 8192
PAGE_SIZE = 128

N_HEADS = 8
HEAD_DIM = D_MODEL // 16
ATTN_W = N_HEADS * HEAD_DIM
IDX_HEADS = 8
IDX_DIM = 64
TOPK_MAX = 256
Q_BLOCK = 128
CONV_CH = D_MODEL // 2
CONV_WIDTH = 31
MEM_LEN = 256
MEM_HEADS = 4
MEM_HEAD_DIM = D_MODEL // 8
MEM_W = MEM_HEADS * MEM_HEAD_DIM
N_BRANCH = 3
D_FF = 2816
N_BUCKETS = 32
MAX_DISTANCE = 128
LN_EPS = 1e-5
ALPHA = (2 * DEPTH) ** 0.25
BETA = (8 * DEPTH) ** -0.25
IN_SPLITS = [ATTN_W, ATTN_W, ATTN_W, IDX_HEADS * IDX_DIM, IDX_DIM, IDX_HEADS, 2 * CONV_CH, MEM_W, N_BRANCH * D_MODEL]
D_IN = sum(IN_SPLITS)
SPLIT_POINTS = [sum(IN_SPLITS[:i + 1]) for i in range(len(IN_SPLITS) - 1)]

kernel_name = 'dsa_conformer_memory_gated_hybrid_step'


def layer_norm(x, g, b):
    xf = x.astype(jnp.float32)
    mu = xf.mean(-1, keepdims=True)
    var = jnp.square(xf - mu).mean(-1, keepdims=True)
    return ((xf - mu) * lax.rsqrt(var + LN_EPS) * g.astype(jnp.float32) + b.astype(jnp.float32)).astype(x.dtype)


def post_norm(x, sub, g, b):
    return layer_norm(ALPHA * x + sub, g, b)


def half_ffn(x, wg, wu, wd):
    return 0.5 * ((jax.nn.silu(x @ wg) * (x @ wu)) @ wd)


def rel_bucket(n):
    n = jnp.maximum(n, 0)
    exact = N_BUCKETS // 2
    nf = jnp.maximum(n, 1).astype(jnp.float32)
    large = exact + (jnp.log(nf / exact) / math.log(MAX_DISTANCE / exact) * (N_BUCKETS - exact)).astype(jnp.int32)
    large = jnp.minimum(large, N_BUCKETS - 1)
    return jnp.where(n < exact, n, large)


def in_project(h, w_in):
    B, T, _ = h.shape
    q, k, v, iq, ik, iw, cu, mq, g = jnp.split(h @ w_in, SPLIT_POINTS, axis=-1)
    hs = (B, T, N_HEADS, HEAD_DIM)
    return (q.reshape(hs), k.reshape(hs), v.reshape(hs), iq.reshape(B, T, IDX_HEADS, IDX_DIM),
            ik, iw, cu, mq.reshape(B, T, MEM_HEADS, MEM_HEAD_DIM), g)


def indexer_scores(iq, iw, ik, tpos):
    dots = jnp.einsum('bqhd,bld->bqhl', iq, ik) * (IDX_DIM ** -0.5)
    s = jnp.einsum('bqhl,bqh->bql', jax.nn.relu(dots), iw * (IDX_HEADS ** -0.5)).astype(jnp.float32)
    causal = jnp.arange(ik.shape[1])[None, None, :] <= tpos[None, :, None]
    return jnp.where(causal, s, -jnp.inf)


def sparse_attend(q, k_sel, v_sel, idx, tpos, rel_bias):
    logits = jnp.einsum('bqhd,bqkhd->bhqk', q, k_sel).astype(jnp.float32) * (HEAD_DIM ** -0.5)
    n = tpos[None, :, None] - idx
    bias = jnp.transpose(rel_bias[rel_bucket(n)], (0, 3, 1, 2)).astype(jnp.float32)
    logits = jnp.where((n >= 0)[:, None], logits + bias, -jnp.inf)
    p = jax.nn.softmax(logits, axis=-1).astype(v_sel.dtype)
    return jnp.einsum('bhqk,bqkhd->bqhd', p, v_sel)


def gather_rows(rows, ids):
    return jax.vmap(lambda r, i: r[i])(rows, ids)


def dsa_prompt(q, k, v, iq, ik, iw, rel_bias):
    B, S = q.shape[:2]
    nb = S // Q_BLOCK
    topk = min(TOPK_MAX, S // 4)

    def blocks(a):
        return jnp.moveaxis(a.reshape((B, nb, Q_BLOCK) + a.shape[2:]), 1, 0)

    def one_block(args):
        qb, iqb, iwb, start = args
        tpos = start + jnp.arange(Q_BLOCK, dtype=jnp.int32)
        _, idx = lax.top_k(indexer_scores(iqb, iwb, ik, tpos), topk)
        return sparse_attend(qb, gather_rows(k, idx), gather_rows(v, idx), idx, tpos, rel_bias)

    starts = jnp.arange(nb, dtype=jnp.int32) * Q_BLOCK
    out = lax.map(one_block, (blocks(q), blocks(iq), blocks(iw), starts))
    return jnp.moveaxis(out, 0, 1).reshape(q.shape)


def dsa_sample(q, k, v, iq, ik, iw, cache_k, cache_v, cache_idx_k, layer, page_table, rel_bias):
    Bd, T = q.shape[:2]
    topk = min(TOPK_MAX, (PAST_LEN + T) // 4)
    tpos = PAST_LEN + jnp.arange(T, dtype=jnp.int32)
    ik_past = cache_idx_k[layer, page_table].reshape(Bd, PAST_LEN, IDX_DIM)
    scores = indexer_scores(iq, iw, jnp.concatenate([ik_past, ik], axis=1), tpos)
    _, idx = lax.top_k(scores, topk)
    in_past = (idx < PAST_LEN)[..., None, None]
    pidx = jnp.minimum(idx, PAST_LEN - 1)
    phys = jnp.take_along_axis(page_table, (pidx // PAGE_SIZE).reshape(Bd, -1), axis=1).reshape(idx.shape)
    off = pidx % PAGE_SIZE
    nidx = jnp.clip(idx - PAST_LEN, 0, T - 1)
    k_sel = jnp.where(in_past, cache_k[layer, phys, off], gather_rows(k, nidx))
    v_sel = jnp.where(in_past, cache_v[layer, phys, off], gather_rows(v, nidx))
    return sparse_attend(q, k_sel, v_sel, idx, tpos, rel_bias)


def conv_module(cu, prev, dw, db, g, b):
    u = cu[..., :CONV_CH] * jax.nn.sigmoid(cu[..., CONV_CH:])
    ext = jnp.concatenate([prev, u], axis=1)
    y = lax.conv_general_dilated(ext, dw[:, None, :], window_strides=(1,), padding='VALID',
                                 dimension_numbers=('NWC', 'WIO', 'NWC'), feature_group_count=CONV_CH) + db
    return jax.nn.silu(layer_norm(y, g, b)), ext[:, -(CONV_WIDTH - 1):]


def mem_kv(mem, w):
    B, M, _ = mem.shape
    mk, mv = jnp.split(mem @ w, 2, axis=-1)
    hs = (B, M, MEM_HEADS, MEM_HEAD_DIM)
    return mk.reshape(hs), mv.reshape(hs)


def mem_attend(q, mk, mv):
    logits = jnp.einsum('bthd,bmhd->bhtm', q, mk).astype(jnp.float32) * (MEM_HEAD_DIM ** -0.5)
    p = jax.nn.softmax(logits, axis=-1).astype(mv.dtype)
    return jnp.einsum('bhtm,bmhd->bthd', p, mv)


def merge_branches(a, c, m, gpre, w_ba, w_bc, w_bm, w_o):
    B, T = a.shape[:2]
    g = jax.nn.sigmoid(gpre).reshape(B, T, N_BRANCH, D_MODEL)
    y = (g[:, :, 0] * (a.reshape(B, T, ATTN_W) @ w_ba)
         + g[:, :, 1] * (c @ w_bc)
         + g[:, :, 2] * (m.reshape(B, T, MEM_W) @ w_bm))
    return y @ w_o


def setup_inputs(seed: int = 0) -> dict:
    key = jax.random.key(seed)
    ks = jax.random.split(key, 32)
    f32 = jnp.float32
    n_pages = PAST_LEN // PAGE_SIZE
    n_phys = (DEC_BATCH * n_pages * 5) // 4

    def nrm(k, shape, scale):
        return jax.random.normal(k, shape, f32) * scale

    page_table = jax.random.permutation(ks[9], n_phys)[:DEC_BATCH * n_pages].reshape(DEC_BATCH, n_pages).astype(jnp.int32)
    return {
        'x_prompt': nrm(ks[0], (BATCH, SEQ, D_MODEL), 1.0),
        'x_sample': nrm(ks[1], (DEC_BATCH, DEC_SEQ, D_MODEL), 1.0),
        'mem_prompt': nrm(ks[2], (BATCH, MEM_LEN, D_MODEL), 1.0),
        'cache_k': nrm(ks[3], (DEPTH, n_phys, PAGE_SIZE, N_HEADS, HEAD_DIM), 1.0),
        'cache_v': nrm(ks[4], (DEPTH, n_phys, PAGE_SIZE, N_HEADS, HEAD_DIM), 1.0),
        'cache_idx_k': nrm(ks[5], (DEPTH, n_phys, PAGE_SIZE, IDX_DIM), 1.0),
        'state_conv': nrm(ks[6], (DEPTH, DEC_BATCH, CONV_WIDTH - 1, CONV_CH), 0.5),
        'cache_mem_k': nrm(ks[7], (DEPTH, DEC_BATCH, MEM_LEN, MEM_HEADS, MEM_HEAD_DIM), 1.0),
        'cache_mem_v': nrm(ks[8], (DEPTH, DEC_BATCH, MEM_LEN, MEM_HEADS, MEM_HEAD_DIM), 1.0),
        'page_table': page_table,
        'ln_g': 1.0 + nrm(ks[10], (DEPTH, 3, D_MODEL), 0.02),
        'ln_b': nrm(ks[11], (DEPTH, 3, D_MODEL), 0.02),
        'w_ff_gate': nrm(ks[12], (DEPTH, 2, D_MODEL, D_FF), D_MODEL ** -0.5),
        'w_ff_up': nrm(ks[13], (DEPTH, 2, D_MODEL, D_FF), D_MODEL ** -0.5),
        'w_ff_down': nrm(ks[14], (DEPTH, 2, D_FF, D_MODEL), BETA * D_FF ** -0.5),
        'w_in': nrm(ks[15], (DEPTH, D_MODEL, D_IN), D_MODEL ** -0.5),
        'conv_dw': nrm(ks[16], (DEPTH, CONV_WIDTH, CONV_CH), CONV_WIDTH ** -0.5),
        'conv_db': nrm(ks[17], (DEPTH, CONV_CH), 0.02),
        'conv_ln_g': 1.0 + nrm(ks[18], (DEPTH, CONV_CH), 0.02),
        'conv_ln_b': nrm(ks[19], (DEPTH, CONV_CH), 0.02),
        'w_mem_kv': nrm(ks[20], (DEPTH, D_MODEL, 2 * MEM_W), D_MODEL ** -0.5),
        'w_br_attn': nrm(ks[21], (DEPTH, ATTN_W, D_MODEL), ATTN_W ** -0.5),
        'w_br_conv': nrm(ks[22], (DEPTH, CONV_CH, D_MODEL), CONV_CH ** -0.5),
        'w_br_mem': nrm(ks[23], (DEPTH, MEM_W, D_MODEL), MEM_W ** -0.5),
        'w_o': nrm(ks[24], (DEPTH, D_MODEL, D_MODEL), BETA * D_MODEL ** -0.5),
        'rel_bias': nrm(ks[25], (N_BUCKETS, N_HEADS), 0.5),
    }


def reference(x_prompt, x_sample, mem_prompt, cache_k, cache_v, cache_idx_k, state_conv, cache_mem_k, cache_mem_v,
              page_table, ln_g, ln_b, w_ff_gate, w_ff_up, w_ff_down, w_in, conv_dw, conv_db, conv_ln_g, conv_ln_b,
              w_mem_kv, w_br_attn, w_br_conv, w_br_mem, w_o, rel_bias):

    def layer(x, l, mix):
        x = post_norm(x, half_ffn(x, w_ff_gate[l, 0], w_ff_up[l, 0], w_ff_down[l, 0]), ln_g[l, 0], ln_b[l, 0])
        q, k, v, iq, ik, iw, cu, mq, gpre = in_project(x, w_in[l])
        a, c, m, extra = mix(q, k, v, iq, ik, iw, cu, mq)
        x = post_norm(x, merge_branches(a, c, m, gpre, w_br_attn[l], w_br_conv[l], w_br_mem[l], w_o[l]),
                      ln_g[l, 1], ln_b[l, 1])
        x = post_norm(x, half_ffn(x, w_ff_gate[l, 1], w_ff_up[l, 1], w_ff_down[l, 1]), ln_g[l, 2], ln_b[l, 2])
        return x, (k, v, ik) + extra

    def prompt_mix(l):
        def mix(q, k, v, iq, ik, iw, cu, mq):
            a = dsa_prompt(q, k, v, iq, ik, iw, rel_bias)
            prev = jnp.zeros((cu.shape[0], CONV_WIDTH - 1, CONV_CH), cu.dtype)
            c, cst = conv_module(cu, prev, conv_dw[l], conv_db[l], conv_ln_g[l], conv_ln_b[l])
            mk, mv = mem_kv(mem_prompt, w_mem_kv[l])
            return a, c, mem_attend(mq, mk, mv), (cst, mk, mv)
        return mix

    def sample_mix(l):
        def mix(q, k, v, iq, ik, iw, cu, mq):
            a = dsa_sample(q, k, v, iq, ik, iw, cache_k, cache_v, cache_idx_k, l, page_table, rel_bias)
            c, cst = conv_module(cu, state_conv[l], conv_dw[l], conv_db[l], conv_ln_g[l], conv_ln_b[l])
            return a, c, mem_attend(mq, cache_mem_k[l], cache_mem_v[l]), (cst,)
        return mix

    xp, xs = x_prompt, x_sample
    st_p, st_s = [], []
    for l in range(DEPTH):
        xp, sp = layer(xp, l, prompt_mix(l))
        xs, ss = layer(xs, l, sample_mix(l))
        st_p.append(sp)
        st_s.append(ss)

    k_prompt = jnp.stack([s[0] for s in st_p])
    v_prompt = jnp.stack([s[1] for s in st_p])
    idx_k_prompt = jnp.stack([s[2] for s in st_p])
    conv_prompt = jnp.stack([s[3] for s in st_p])
    mem_k_prompt = jnp.stack([s[4] for s in st_p])
    mem_v_prompt = jnp.stack([s[5] for s in st_p])
    k_sample = jnp.stack([s[0] for s in st_s])
    v_sample = jnp.stack([s[1] for s in st_s])
    idx_k_sample = jnp.stack([s[2] for s in st_s])
    conv_sample = jnp.stack([s[3] for s in st_s])
    return (xp, xs, k_prompt, v_prompt, idx_k_prompt, conv_prompt, mem_k_prompt, mem_v_prompt,
            k_sample, v_sample, idx_k_sample, conv_sample)
```

```python
import functools
import math

import numpy as np
import jax
import jax.numpy as jnp
from jax import lax
from jax.experimental import pallas as pl
from jax.experimental.pallas import tpu as pltpu

D_MODEL = 1024
N_HEADS = 8
HEAD_DIM = 64
ATTN_W = 512
IDX_HEADS = 8
IDX_DIM = 64
TOPK = 256
CONV_CH = 512
CONV_WIDTH = 31
MEM_LEN = 256
MEM_HEADS = 4
MEM_HEAD_DIM = 128
D_FF = 2816
N_BUCKETS = 32
MAX_DISTANCE = 128
LN_EPS = 1e-5
DEPTH = 2
ALPHA = (2 * DEPTH) ** 0.25
PAGE_SIZE = 128
PAST_LEN = 8192

LANES = 128
SUBLANES = 8
VMEM_LIMIT = 56 * 1024 * 1024

FF_CHUNK = 256
N_FF_CHUNKS = D_FF // FF_CHUNK
TQ = 256
STATE_ROWS = 32
STATE_PAD = STATE_ROWS - (CONV_WIDTH - 1)
SAMPLE_T_PAD = 8
NEW_PAD = 128
KV_GROUP_PAGES = 8
KV_GROUP = KV_GROUP_PAGES * PAGE_SIZE

IDX_SCALE = (IDX_DIM ** -0.5) * (IDX_HEADS ** -0.5)
INT_MIN = -(2 ** 31)
KEY_NEG_INF = -2139095041
NEG_INF = float("-inf")

bf16 = jnp.bfloat16
f32 = jnp.float32


def _dot(a, b):
    return jnp.dot(a, b, preferred_element_type=f32)


def _dot_nt(a, b):
    return lax.dot_general(a, b, (((1,), (1,)), ((), ())), preferred_element_type=f32)


def _sigmoid(x):
    return 1.0 / (1.0 + jnp.exp(-x))


def _layer_norm(y, g, b):
    mu = jnp.mean(y, axis=-1, keepdims=True)
    d = y - mu
    var = jnp.mean(d * d, axis=-1, keepdims=True)
    return d * lax.rsqrt(var + LN_EPS) * g + b


def _key_to_float(key):
    bits = key ^ (lax.shift_right_arithmetic(key, 31) & 0x7FFFFFFF)
    return lax.bitcast_convert_type(bits, f32)


def _params(sem):
    return pltpu.CompilerParams(dimension_semantics=sem, vmem_limit_bytes=VMEM_LIMIT)


def _whole(shape):
    nd = len(shape)
    return pl.BlockSpec(shape, lambda *_: (0,) * nd)


def _ffn_ln_kernel(x_ref, wg_ref, wu_ref, wd_ref, g_ref, b_ref, o_ref, acc_ref):
    x = x_ref[...]
    xb = x.astype(bf16)
    acc_ref[...] = jnp.zeros_like(acc_ref)

    def chunk(c, carry):
        gate = _dot(xb, wg_ref[c])
        up = _dot(xb, wu_ref[c])
        h = (gate * _sigmoid(gate) * up).astype(bf16)
        acc_ref[...] += _dot(h, wd_ref[c])
        return carry

    lax.fori_loop(0, N_FF_CHUNKS, chunk, 0)
    y = ALPHA * x + 0.5 * acc_ref[...]
    o_ref[...] = _layer_norm(y, g_ref[...], b_ref[...])


def _ffn_ln(x, wg, wu, wd, g, b, tm):
    n = x.shape[0]
    return pl.pallas_call(
        _ffn_ln_kernel,
        grid=(n // tm,),
        in_specs=[pl.BlockSpec((tm, D_MODEL), lambda i: (i, 0)),
                  _whole(wg.shape), _whole(wu.shape), _whole(wd.shape),
                  _whole(g.shape), _whole(b.shape)],
        out_specs=pl.BlockSpec((tm, D_MODEL), lambda i: (i, 0)),
        out_shape=jax.ShapeDtypeStruct((n, D_MODEL), f32),
        scratch_shapes=[pltpu.VMEM((tm, D_MODEL), f32)],
        compiler_params=_params(("parallel",)),
        name="ffn_ln",
    )(x, wg, wu, wd, g, b)


def _proj_kernel(n_out, x_ref, *refs):
    xb = x_ref[...].astype(bf16)
    for w_ref, o_ref in zip(refs[:n_out], refs[n_out:]):
        o_ref[...] = _dot(xb, w_ref[...])


def _proj(x, ws, tm):
    n = x.shape[0]
    return pl.pallas_call(
        functools.partial(_proj_kernel, len(ws)),
        grid=(n // tm,),
        in_specs=[pl.BlockSpec((tm, D_MODEL), lambda i: (i, 0))] + [_whole(w.shape) for w in ws],
        out_specs=[pl.BlockSpec((tm, w.shape[1]), lambda i: (i, 0)) for w in ws],
        out_shape=[jax.ShapeDtypeStruct((n, w.shape[1]), f32) for w in ws],
        compiler_params=_params(("parallel",)),
        name="in_proj",
    )(x, *ws)


def _conv_kernel(tt, n_tiles, a_ref, b_ref, prev_ref, dw_ref, db_ref, g_ref, bb_ref,
                 c_ref, st_ref, ext_ref):
    j = pl.program_id(1)

    @pl.when(j == 0)
    def _():
        ext_ref[0:STATE_ROWS, :] = prev_ref[0]

    ext_ref[STATE_ROWS:STATE_ROWS + tt, :] = a_ref[0] * _sigmoid(b_ref[0])
    acc = jnp.zeros((tt, CONV_CH), f32)
    for tap in range(CONV_WIDTH):
        acc = acc + ext_ref[STATE_PAD + tap:STATE_PAD + tap + tt, :] * dw_ref[tap:tap + 1, :]
    y = _layer_norm(acc + db_ref[...], g_ref[...], bb_ref[...])
    c_ref[0] = y * _sigmoid(y)
    tail = ext_ref[tt:tt + STATE_ROWS, :]
    st_ref[0] = tail
    if n_tiles > 1:
        ext_ref[0:STATE_ROWS, :] = tail


def _conv_module(a, b, prev, dw, db, g, bb, tt):
    bsz, t, _ = a.shape
    n_tiles = t // tt
    blk = pl.BlockSpec((1, tt, CONV_CH), lambda i, j: (i, j, 0))
    st = pl.BlockSpec((1, STATE_ROWS, CONV_CH), lambda i, j: (i, 0, 0))
    return pl.pallas_call(
        functools.partial(_conv_kernel, tt, n_tiles),
        grid=(bsz, n_tiles),
        in_specs=[blk, blk, st, _whole(dw.shape), _whole(db.shape), _whole(g.shape), _whole(bb.shape)],
        out_specs=[blk, st],
        out_shape=[jax.ShapeDtypeStruct((bsz, t, CONV_CH), f32),
                   jax.ShapeDtypeStruct((bsz, STATE_ROWS, CONV_CH), f32)],
        scratch_shapes=[pltpu.VMEM((STATE_ROWS + tt, CONV_CH), f32)],
        compiler_params=_params(("parallel", "arbitrary")),
        name="conv_module",
    )(a, b, prev, dw, db, g, bb)


def _mem_attn_kernel(q_ref, mk_ref, mv_ref, o_ref):
    scale = MEM_HEAD_DIM ** -0.5
    for h in range(MEM_HEADS):
        sl = slice(h * MEM_HEAD_DIM, (h + 1) * MEM_HEAD_DIM)
        qh = (q_ref[0, :, sl] * scale).astype(bf16)
        kh = mk_ref[0, :, sl].astype(bf16)
        vh = mv_ref[0, :, sl].astype(bf16)
        lg = _dot_nt(qh, kh)
        mx = jnp.max(lg, axis=-1, keepdims=True)
        p = jnp.exp(lg - mx)
        den = jnp.sum(p, axis=-1, keepdims=True)
        o_ref[0, :, sl] = _dot((p / den).astype(bf16), vh)


def _mem_attn(q, mk, mv, tq):
    bsz, t, w = q.shape
    qblk = pl.BlockSpec((1, tq, w), lambda i, j: (i, j, 0))
    mblk = pl.BlockSpec((1, MEM_LEN, w), lambda i, j: (i, 0, 0))
    return pl.pallas_call(
        _mem_attn_kernel,
        grid=(bsz, t // tq),
        in_specs=[qblk, mblk, mblk],
        out_specs=qblk,
        out_shape=jax.ShapeDtypeStruct((bsz, t, w), f32),
        compiler_params=_params(("parallel", "parallel")),
        name="mem_attn",
    )(q, mk, mv)


def _merge_ln_kernel(x_ref, a_ref, c_ref, m_ref, g0_ref, g1_ref, g2_ref,
                     wa_ref, wc_ref, wm_ref, wo_ref, g_ref, b_ref, o_ref):
    y = _sigmoid(g0_ref[...]) * _dot(a_ref[...].astype(bf16), wa_ref[...])
    y = y + _sigmoid(g1_ref[...]) * _dot(c_ref[...].astype(bf16), wc_ref[...])
    y = y + _sigmoid(g2_ref[...]) * _dot(m_ref[...].astype(bf16), wm_ref[...])
    z = _dot(y.astype(bf16), wo_ref[...])
    o_ref[...] = _layer_norm(ALPHA * x_ref[...] + z, g_ref[...], b_ref[...])


def _merge_ln(x, a, c, m, g0, g1, g2, wa, wc, wm, wo, g, b, tm):
    n = x.shape[0]
    wide = pl.BlockSpec((tm, D_MODEL), lambda i: (i, 0))
    half = pl.BlockSpec((tm, ATTN_W), lambda i: (i, 0))
    return pl.pallas_call(
        _merge_ln_kernel,
        grid=(n // tm,),
        in_specs=[wide, half, half, half, wide, wide, wide,
                  _whole(wa.shape), _whole(wc.shape), _whole(wm.shape), _whole(wo.shape),
                  _whole(g.shape), _whole(b.shape)],
        out_specs=wide,
        out_shape=jax.ShapeDtypeStruct((n, D_MODEL), f32),
        compiler_params=_params(("parallel",)),
        name="merge_ln",
    )(x, a, c, m, g0, g1, g2, wa, wc, wm, wo, g, b)


def _rel_bucket_np(n):
    n = np.maximum(n, 0)
    exact = N_BUCKETS // 2
    nf = np.maximum(n, 1).astype(np.float64)
    large = exact + (np.log(nf / exact) / math.log(MAX_DISTANCE / exact) * (N_BUCKETS - exact)).astype(np.int64)
    large = np.minimum(large, N_BUCKETS - 1)
    return np.where(n < exact, n, large).astype(np.int32)


def _bias_kernel(rb_ref, bkt_ref, o_ref):
    n_tiles = bkt_ref.shape[0]
    for t in range(n_tiles):
        bkt = bkt_ref[t]
        for h in range(N_HEADS):
            far = rb_ref[N_BUCKETS - 1, h]
            acc = jnp.zeros(bkt.shape, f32)
            for bucket in range(N_BUCKETS - 1):
                acc = jnp.where(bkt == bucket, rb_ref[bucket, h] - far, acc)
            o_ref[t, h] = acc


def _bias_tiles(rel_bias, buckets):
    n_tiles, r, c = buckets.shape
    return pl.pallas_call(
        _bias_kernel,
        in_specs=[pl.BlockSpec(memory_space=pltpu.SMEM), _whole(buckets.shape)],
        out_specs=_whole((n_tiles, N_HEADS, r, c)),
        out_shape=jax.ShapeDtypeStruct((n_tiles, N_HEADS, r, c), f32),
        grid=(1,),
        name="rel_bias_tiles",
    )(rel_bias, buckets)


def _softmax_step(lg, m_ref, l_ref, acc_ref, h, vmat):
    m_old = m_ref[h]
    m_new = jnp.maximum(m_old, jnp.max(lg, axis=-1, keepdims=True))
    m_safe = jnp.where(m_new == NEG_INF, 0.0, m_new)
    alpha = jnp.exp(m_old - m_safe)
    p = jnp.exp(lg - m_safe[:, 0:1])
    l_ref[h] = alpha * l_ref[h] + jnp.sum(p, axis=-1, keepdims=True)
    acc_ref[h] = alpha[:, 0:vmat.shape[-1]] * acc_ref[h] + _dot(p.astype(bf16), vmat)
    m_ref[h] = m_new


def _dsa_prompt_kernel(q_ref, iq_ref, ikw_ref, kt_ref, v_ref, ikt_ref, bias_ref, o_ref,
                       s_ref, w_ref, m_ref, l_ref, acc_ref):
    j = pl.program_id(1)
    n_chunks = j + 1
    row = lax.broadcasted_iota(jnp.int32, (TQ, TQ), 0)
    col = lax.broadcasted_iota(jnp.int32, (TQ, TQ), 1)

    ikw = ikw_ref[0]
    for h in range(IDX_HEADS):
        w_ref[h] = jnp.broadcast_to(ikw[:, IDX_DIM + h:IDX_DIM + h + 1] * IDX_SCALE, (TQ, LANES))

    def score_chunk(c, carry):
        ks = pl.multiple_of(c * TQ, TQ)
        s = jnp.zeros((TQ, TQ), f32)
        for h in range(IDX_HEADS):
            d = _dot(iq_ref[0, h], ikt_ref[0, :, pl.ds(ks, TQ)])
            wh = w_ref[h]
            s = s + jnp.maximum(d, 0.0) * jnp.concatenate([wh, wh], axis=1)
        s = jnp.where(col + (c - j) * TQ <= row, s, NEG_INF)
        s_ref[:, pl.ds(ks, TQ)] = s
        return carry

    lax.fori_loop(0, n_chunks, score_chunk, 0)

    def count_ge(thr):
        thr2 = jnp.concatenate([thr, thr], axis=1)

        def body(c, cnt):
            ks = pl.multiple_of(c * TQ, TQ)
            hit = jnp.where(s_ref[:, pl.ds(ks, TQ)] >= thr2, 1, 0)
            return cnt + hit[:, :LANES] + hit[:, LANES:]

        cnt = lax.fori_loop(0, n_chunks, body, jnp.zeros((TQ, LANES), jnp.int32))
        return jnp.broadcast_to(jnp.sum(cnt, axis=-1, keepdims=True), (TQ, LANES))

    def search(i, carry):
        t_key, t_cnt = carry
        step = lax.shift_left(jnp.int32(1), 31 - i)
        cand = t_key + step
        cnt = count_ge(_key_to_float(cand))
        ok = cnt >= TOPK
        return jnp.where(ok, cand, t_key), jnp.where(ok, cnt, t_cnt)

    t_key, t_cnt = lax.fori_loop(
        0, 32, search,
        (jnp.full((TQ, LANES), INT_MIN, jnp.int32), jnp.full((TQ, LANES), 1 << 30, jnp.int32)))
    has_k = t_key > KEY_NEG_INF
    thr = _key_to_float(jnp.maximum(t_key, KEY_NEG_INF + 1))

    tied = jnp.logical_and(has_k, t_cnt > TOPK)

    @pl.when(jnp.max(jnp.where(tied, 1, 0)) > 0)
    def _():
        thr2 = jnp.concatenate([thr, thr], axis=1)

        def count_eq_below(pos):
            pos2 = jnp.concatenate([pos, pos], axis=1)

            def body(c, cnt):
                ks = pl.multiple_of(c * TQ, TQ)
                hit = jnp.where(jnp.logical_and(s_ref[:, pl.ds(ks, TQ)] == thr2, col + c * TQ < pos2), 1, 0)
                return cnt + hit[:, :LANES] + hit[:, LANES:]

            cnt = lax.fori_loop(0, n_chunks, body, jnp.zeros((TQ, LANES), jnp.int32))
            return jnp.broadcast_to(jnp.sum(cnt, axis=-1, keepdims=True), (TQ, LANES))

        n_eq = count_eq_below(jnp.full((TQ, LANES), 1 << 30, jnp.int32))
        need = jnp.where(tied, TOPK - (t_cnt - n_eq), 1 << 30)

        def psearch(i, pos):
            cand = pos + lax.shift_left(jnp.int32(1), 12 - i)
            return jnp.where(count_eq_below(cand) <= need - 1, cand, pos)

        pmax = lax.fori_loop(0, 13, psearch, jnp.zeros((TQ, LANES), jnp.int32))
        pmax2 = jnp.concatenate([pmax, pmax], axis=1)

        def drop(c, carry):
            ks = pl.multiple_of(c * TQ, TQ)
            s = s_ref[:, pl.ds(ks, TQ)]
            s_ref[:, pl.ds(ks, TQ)] = jnp.where(
                jnp.logical_and(s == thr2, col + c * TQ > pmax2), NEG_INF, s)
            return carry

        lax.fori_loop(0, n_chunks, drop, 0)

    m_ref[...] = jnp.full(m_ref.shape, NEG_INF, f32)
    l_ref[...] = jnp.zeros(l_ref.shape, f32)
    acc_ref[...] = jnp.zeros(acc_ref.shape, f32)
    thr2 = jnp.concatenate([thr, thr], axis=1)
    scale = HEAD_DIM ** -0.5

    def attend_chunk(c, bias_tile):
        ks = pl.multiple_of(c * TQ, TQ)
        sel = s_ref[:, pl.ds(ks, TQ)] >= thr2
        for h in range(N_HEADS):
            qh = (q_ref[0, h].astype(f32) * scale).astype(bf16)
            lg = _dot(qh, kt_ref[0, h * HEAD_DIM:(h + 1) * HEAD_DIM, pl.ds(ks, TQ)])
            if bias_tile is not None:
                lg = lg + bias_ref[bias_tile, h]
            lg = jnp.where(sel, lg, NEG_INF)
            pair = (h // 2) * LANES
            _softmax_step(lg, m_ref, l_ref, acc_ref, h, v_ref[0, pl.ds(ks, TQ), pair:pair + LANES])

    def far_chunk(c, carry):
        attend_chunk(c, None)
        return carry

    lax.fori_loop(0, jnp.maximum(j - 1, 0), far_chunk, 0)

    @pl.when(j >= 1)
    def _():
        attend_chunk(j - 1, 1)

    attend_chunk(j, 0)

    lane = lax.broadcasted_iota(jnp.int32, (TQ, LANES), 1)
    for pair in range(N_HEADS // 2):
        even = acc_ref[2 * pair] / l_ref[2 * pair]
        odd = acc_ref[2 * pair + 1] / l_ref[2 * pair + 1]
        o_ref[0, :, pair * LANES:(pair + 1) * LANES] = jnp.where(lane < HEAD_DIM, even, odd)


def _dsa_prompt(qh, iqh, ikw, kt, vb, ikt, bias):
    bsz, _, s, _ = qh.shape
    once = pl.Buffered(1)
    return pl.pallas_call(
        _dsa_prompt_kernel,
        grid=(bsz, s // TQ),
        in_specs=[pl.BlockSpec((1, N_HEADS, TQ, HEAD_DIM), lambda b, j: (b, 0, j, 0)),
                  pl.BlockSpec((1, IDX_HEADS, TQ, IDX_DIM), lambda b, j: (b, 0, j, 0)),
                  pl.BlockSpec((1, TQ, LANES), lambda b, j: (b, j, 0)),
                  pl.BlockSpec((1, ATTN_W, s), lambda b, j: (b, 0, 0), pipeline_mode=once),
                  pl.BlockSpec((1, s, ATTN_W), lambda b, j: (b, 0, 0), pipeline_mode=once),
                  pl.BlockSpec((1, IDX_DIM, s), lambda b, j: (b, 0, 0), pipeline_mode=once),
                  _whole(bias.shape)],
        out_specs=pl.BlockSpec((1, TQ, ATTN_W), lambda b, j: (b, j, 0)),
        out_shape=jax.ShapeDtypeStruct((bsz, s, ATTN_W), f32),
        scratch_shapes=[pltpu.VMEM((TQ, s), f32),
                        pltpu.VMEM((IDX_HEADS, TQ, LANES), f32),
                        pltpu.VMEM((N_HEADS, TQ, LANES), f32),
                        pltpu.VMEM((N_HEADS, TQ, LANES), f32),
                        pltpu.VMEM((N_HEADS, TQ, LANES), f32)],
        compiler_params=_params(("parallel", "arbitrary")),
        name="dsa_prompt",
    )(qh, iqh, ikw, kt, vb, ikt, bias)


N_PAGES = PAST_LEN // PAGE_SIZE
N_KV_GROUPS = N_PAGES // KV_GROUP_PAGES
S_KEYS = PAST_LEN + NEW_PAD


def _dsa_sample_kernel(layer, pt_ref, q_ref, iq_ref, ikw_ref, kn_ref, vn_ref, ikn_ref,
                       bias_ref, cik_ref, ck_ref, cv_ref, o_ref,
                       ikbuf, kbuf, vbuf, s_ref, m_ref, l_ref, acc_ref, sem_ik, sem_kv):
    b = pl.program_id(0)
    tp = SAMPLE_T_PAD

    def ik_copy(p, phys):
        return pltpu.make_async_copy(cik_ref.at[layer, phys],
                                     ikbuf.at[pl.ds(pl.multiple_of(p * PAGE_SIZE, PAGE_SIZE), PAGE_SIZE), :],
                                     sem_ik.at[0])

    def kv_copies(slot, g, i, phys):
        dst = pl.ds(pl.multiple_of(i * PAGE_SIZE, PAGE_SIZE), PAGE_SIZE)
        out = []
        for h in range(N_HEADS):
            out.append(pltpu.make_async_copy(ck_ref.at[layer, phys, :, h, :],
                                             kbuf.at[slot, h, dst, :], sem_kv.at[slot]))
            out.append(pltpu.make_async_copy(cv_ref.at[layer, phys, :, h, :],
                                             vbuf.at[slot, h, dst, :], sem_kv.at[slot]))
        return out

    def start_group(g, slot):
        def body(i, carry):
            for cp in kv_copies(slot, g, i, pt_ref[b, g * KV_GROUP_PAGES + i]):
                cp.start()
            return carry
        lax.fori_loop(0, KV_GROUP_PAGES, body, 0)

    def wait_group(slot):
        def body(i, carry):
            for cp in kv_copies(slot, 0, i, 0):
                cp.wait()
            return carry
        lax.fori_loop(0, KV_GROUP_PAGES, body, 0)

    def start_ik(p, carry):
        ik_copy(p, pt_ref[b, p]).start()
        return carry

    def wait_ik(p, carry):
        ik_copy(p, 0).wait()
        return carry

    lax.fori_loop(0, N_PAGES, start_ik, 0)
    start_group(0, 0)
    lax.fori_loop(0, N_PAGES, wait_ik, 0)

    ikw = ikw_ref[0]
    row = lax.broadcasted_iota(jnp.int32, (tp, NEW_PAD), 0)
    col = lax.broadcasted_iota(jnp.int32, (tp, NEW_PAD), 1)
    for g in range(N_KV_GROUPS):
        ikc = ikbuf[g * KV_GROUP:(g + 1) * KV_GROUP, :].astype(bf16)
        s = jnp.zeros((tp, KV_GROUP), f32)
        for h in range(IDX_HEADS):
            d = _dot_nt(iq_ref[0, h], ikc)
            s = s + jnp.maximum(d, 0.0) * (ikw[:, IDX_DIM + h:IDX_DIM + h + 1] * IDX_SCALE)
        s_ref[:, g * KV_GROUP:(g + 1) * KV_GROUP] = s
    s = jnp.zeros((tp, NEW_PAD), f32)
    for h in range(IDX_HEADS):
        d = _dot_nt(iq_ref[0, h], ikn_ref[0])
        s = s + jnp.maximum(d, 0.0) * (ikw[:, IDX_DIM + h:IDX_DIM + h + 1] * IDX_SCALE)
    s_ref[:, PAST_LEN:] = jnp.where(col <= row, s, NEG_INF)

    def count_ge(thr):
        hit = jnp.where(s_ref[...] >= thr, 1, 0)
        return jnp.sum(hit, axis=-1, keepdims=True)

    def search(i, carry):
        t_key, t_cnt = carry
        cand = t_key + lax.shift_left(jnp.int32(1), 31 - i)
        cnt = count_ge(_key_to_float(cand))
        ok = cnt >= TOPK
        return jnp.where(ok, cand, t_key), jnp.where(ok, cnt, t_cnt)

    t_key, t_cnt = lax.fori_loop(
        0, 32, search,
        (jnp.full((tp, 1), INT_MIN, jnp.int32), jnp.full((tp, 1), 1 << 30, jnp.int32)))
    has_k = t_key > KEY_NEG_INF
    thr = _key_to_float(jnp.maximum(t_key, KEY_NEG_INF + 1))
    tied = jnp.logical_and(has_k, t_cnt > TOPK)

    @pl.when(jnp.max(jnp.where(tied, 1, 0)) > 0)
    def _():
        kpos = lax.broadcasted_iota(jnp.int32, (tp, S_KEYS), 1)

        def count_eq_below(pos):
            hit = jnp.where(jnp.logical_and(s_ref[...] == thr, kpos < pos), 1, 0)
            return jnp.sum(hit, axis=-1, keepdims=True)

        n_eq = count_eq_below(jnp.full((tp, 1), 1 << 30, jnp.int32))
        need = jnp.where(tied, TOPK - (t_cnt - n_eq), 1 << 30)

        def psearch(i, pos):
            cand = pos + lax.shift_left(jnp.int32(1), 13 - i)
            return jnp.where(count_eq_below(cand) <= need - 1, cand, pos)

        pmax = lax.fori_loop(0, 14, psearch, jnp.zeros((tp, 1), jnp.int32))
        s = s_ref[...]
        s_ref[...] = jnp.where(jnp.logical_and(s == thr, kpos > pmax), NEG_INF, s)

    m_ref[...] = jnp.full(m_ref.shape, NEG_INF, f32)
    l_ref[...] = jnp.zeros(l_ref.shape, f32)
    acc_ref[...] = jnp.zeros(acc_ref.shape, f32)
    scale = HEAD_DIM ** -0.5

    def group(g, carry):
        slot = g % 2

        @pl.when(g + 1 < N_KV_GROUPS)
        def _():
            start_group(g + 1, 1 - slot)

        wait_group(slot)
        ks = pl.multiple_of(g * KV_GROUP, KV_GROUP)
        sel = s_ref[:, pl.ds(ks, KV_GROUP)] >= thr
        near = jnp.where(g == N_KV_GROUPS - 1, 1.0, 0.0)
        for h in range(N_HEADS):
            qh = (q_ref[0, h].astype(f32) * scale).astype(bf16)
            lg = _dot_nt(qh, kbuf[slot, h].astype(bf16)) + near * bias_ref[0, h]
            lg = jnp.where(sel, lg, NEG_INF)
            _softmax_step(lg, m_ref, l_ref, acc_ref, h, vbuf[slot, h].astype(bf16))
        return carry

    lax.fori_loop(0, N_KV_GROUPS, group, 0)

    sel = s_ref[:, PAST_LEN:] >= thr
    for h in range(N_HEADS):
        qh = (q_ref[0, h].astype(f32) * scale).astype(bf16)
        lg = _dot_nt(qh, kn_ref[0, h]) + bias_ref[1, h, :, 0:NEW_PAD]
        lg = jnp.where(sel, lg, NEG_INF)
        _softmax_step(lg, m_ref, l_ref, acc_ref, h, vn_ref[0, h])
        o_ref[0, :, h * HEAD_DIM:(h + 1) * HEAD_DIM] = acc_ref[h] / l_ref[h][:, 0:HEAD_DIM]


def _dsa_sample(layer, page_table, qh, iqh, ikw, kn, vn, ikn, bias, cache_idx_k, cache_k, cache_v):
    bsz = qh.shape[0]
    tp = SAMPLE_T_PAD
    any_spec = pl.BlockSpec(memory_space=pl.ANY)
    grid_spec = pltpu.PrefetchScalarGridSpec(
        num_scalar_prefetch=1,
        grid=(bsz,),
        in_specs=[pl.BlockSpec((1, N_HEADS, tp, HEAD_DIM), lambda b, pt: (b, 0, 0, 0)),
                  pl.BlockSpec((1, IDX_HEADS, tp, IDX_DIM), lambda b, pt: (b, 0, 0, 0)),
                  pl.BlockSpec((1, tp, LANES), lambda b, pt: (b, 0, 0)),
                  pl.BlockSpec((1, N_HEADS, NEW_PAD, HEAD_DIM), lambda b, pt: (b, 0, 0, 0)),
                  pl.BlockSpec((1, N_HEADS, NEW_PAD, HEAD_DIM), lambda b, pt: (b, 0, 0, 0)),
                  pl.BlockSpec((1, NEW_PAD, IDX_DIM), lambda b, pt: (b, 0, 0)),
                  pl.BlockSpec(bias.shape, lambda b, pt: (0, 0, 0, 0)),
                  any_spec, any_spec, any_spec],
        out_specs=pl.BlockSpec((1, tp, ATTN_W), lambda b, pt: (b, 0, 0)),
        scratch_shapes=[pltpu.VMEM((PAST_LEN, IDX_DIM), f32),
                        pltpu.VMEM((2, N_HEADS, KV_GROUP, HEAD_DIM), f32),
                        pltpu.VMEM((2, N_HEADS, KV_GROUP, HEAD_DIM), f32),
                        pltpu.VMEM((tp, S_KEYS), f32),
                        pltpu.VMEM((N_HEADS, tp, LANES), f32),
                        pltpu.VMEM((N_HEADS, tp, LANES), f32),
                        pltpu.VMEM((N_HEADS, tp, HEAD_DIM), f32),
                        pltpu.SemaphoreType.DMA((1,)),
                        pltpu.SemaphoreType.DMA((2,))],
    )
    return pl.pallas_call(
        functools.partial(_dsa_sample_kernel, layer),
        grid_spec=grid_spec,
        out_shape=jax.ShapeDtypeStruct((bsz, tp, ATTN_W), f32),
        compiler_params=_params(("arbitrary",)),
        name="dsa_sample",
    )(page_table, qh, iqh, ikw, kn, vn, ikn, bias, cache_idx_k, cache_k, cache_v)


def _prompt_buckets():
    q = np.arange(TQ)[:, None]
    k = np.arange(TQ)[None, :]
    return np.stack([_rel_bucket_np(q - k), _rel_bucket_np(TQ + q - k)])


def _sample_buckets():
    t = np.arange(SAMPLE_T_PAD)[:, None]
    k = np.arange(KV_GROUP)[None, :]
    last = _rel_bucket_np(PAST_LEN + t - ((N_KV_GROUPS - 1) * KV_GROUP + k))
    new = np.full((SAMPLE_T_PAD, KV_GROUP), N_BUCKETS - 1, np.int32)
    new[:, :NEW_PAD] = _rel_bucket_np(t - np.arange(NEW_PAD)[None, :])
    return np.stack([last, new])


def _heads_major(x, bsz, t, heads, dim):
    return x.reshape(bsz, t, heads, dim).transpose(0, 2, 1, 3)


def kernel(x_prompt, x_sample, mem_prompt, cache_k, cache_v, cache_idx_k, state_conv, cache_mem_k, cache_mem_v, page_table, ln_g, ln_b, w_ff_gate, w_ff_up, w_ff_down, w_in, conv_dw, conv_db, conv_ln_g, conv_ln_b, w_mem_kv, w_br_attn, w_br_conv, w_br_mem, w_o, rel_bias):
    bp, sp, _ = x_prompt.shape
    bs, ts, _ = x_sample.shape
    depth = w_in.shape[0]

    wg = w_ff_gate.astype(bf16).reshape(depth, 2, D_MODEL, N_FF_CHUNKS, FF_CHUNK).transpose(0, 1, 3, 2, 4)
    wu = w_ff_up.astype(bf16).reshape(depth, 2, D_MODEL, N_FF_CHUNKS, FF_CHUNK).transpose(0, 1, 3, 2, 4)
    wd = w_ff_down.astype(bf16).reshape(depth, 2, N_FF_CHUNKS, FF_CHUNK, D_MODEL)
    w_inb = w_in.astype(bf16)
    edges = np.cumsum([0, ATTN_W, ATTN_W, ATTN_W, IDX_HEADS * IDX_DIM, IDX_DIM + IDX_HEADS,
                       CONV_CH, CONV_CH, MEM_HEADS * MEM_HEAD_DIM, D_MODEL, D_MODEL, D_MODEL])

    def in_weights(l):
        ws = [w_inb[l, :, edges[i]:edges[i + 1]] for i in range(len(edges) - 1)]
        ws[4] = jnp.pad(ws[4], ((0, 0), (0, LANES - ws[4].shape[1])))
        return ws

    w_memb = w_mem_kv.astype(bf16)
    wab, wcb, wmb, wob = (w.astype(bf16) for w in (w_br_attn, w_br_conv, w_br_mem, w_o))
    lng = ln_g.reshape(depth, 3, 1, D_MODEL)
    lnb = ln_b.reshape(depth, 3, 1, D_MODEL)

    bias_p = _bias_tiles(rel_bias, jnp.asarray(_prompt_buckets()))
    bias_s = _bias_tiles(rel_bias, jnp.asarray(_sample_buckets()))

    def layer(x, l, bsz, t, tm, mix):
        x = _ffn_ln(x, wg[l, 0], wu[l, 0], wd[l, 0], lng[l, 0], lnb[l, 0], tm)
        q, k, v, iq, ikw, cua, cub, mq, g0, g1, g2 = _proj(x, in_weights(l), min(tm, 256))
        a, c, cst, m, extra = mix(q, k, v, iq, ikw, cua, cub, mq)
        x = _merge_ln(x, a, c, m, g0, g1, g2, wab[l], wcb[l], wmb[l], wob[l],
                      lng[l, 1], lnb[l, 1], min(tm, 256))
        x = _ffn_ln(x, wg[l, 1], wu[l, 1], wd[l, 1], lng[l, 2], lnb[l, 2], tm)
        state = (k.reshape(bsz, t, N_HEADS, HEAD_DIM), v.reshape(bsz, t, N_HEADS, HEAD_DIM),
                 ikw[:, :IDX_DIM].reshape(bsz, t, IDX_DIM), cst[:, STATE_PAD:]) + extra
        return x, state

    def conv_args(l):
        return (conv_dw[l], conv_db[l].reshape(1, CONV_CH), conv_ln_g[l].reshape(1, CONV_CH),
                conv_ln_b[l].reshape(1, CONV_CH))

    def prompt_mix(l):
        def mix(q, k, v, iq, ikw, cua, cub, mq):
            qh = _heads_major(q, bp, sp, N_HEADS, HEAD_DIM).astype(bf16)
            iqh = _heads_major(iq, bp, sp, IDX_HEADS, IDX_DIM).astype(bf16)
            kt = k.reshape(bp, sp, ATTN_W).transpose(0, 2, 1).astype(bf16)
            vb = v.reshape(bp, sp, ATTN_W).astype(bf16)
            ikt = ikw[:, :IDX_DIM].reshape(bp, sp, IDX_DIM).transpose(0, 2, 1).astype(bf16)
            a = _dsa_prompt(qh, iqh, ikw.reshape(bp, sp, LANES), kt, vb, ikt, bias_p)
            prev = jnp.zeros((bp, STATE_ROWS, CONV_CH), f32)
            c, cst = _conv_module(cua.reshape(bp, sp, CONV_CH), cub.reshape(bp, sp, CONV_CH), prev,
                                  *conv_args(l), 512)
            mem = mem_prompt.reshape(bp * MEM_LEN, D_MODEL)
            mk, mv = _proj(mem, [w_memb[l, :, :ATTN_W], w_memb[l, :, ATTN_W:]], 256)
            m = _mem_attn(mq.reshape(bp, sp, ATTN_W), mk.reshape(bp, MEM_LEN, ATTN_W),
                          mv.reshape(bp, MEM_LEN, ATTN_W), 512)
            extra = (mk.reshape(bp, MEM_LEN, MEM_HEADS, MEM_HEAD_DIM),
                     mv.reshape(bp, MEM_LEN, MEM_HEADS, MEM_HEAD_DIM))
            return (a.reshape(bp * sp, ATTN_W), c.reshape(bp * sp, CONV_CH), cst,
                    m.reshape(bp * sp, ATTN_W), extra)
        return mix

    def pad_rows(x, axis, rows):
        pad = [(0, 0)] * x.ndim
        pad[axis] = (0, rows - x.shape[axis])
        return jnp.pad(x, pad)

    def sample_mix(l):
        def mix(q, k, v, iq, ikw, cua, cub, mq):
            tp = SAMPLE_T_PAD
            qh = pad_rows(_heads_major(q, bs, ts, N_HEADS, HEAD_DIM), 2, tp).astype(bf16)
            iqh = pad_rows(_heads_major(iq, bs, ts, IDX_HEADS, IDX_DIM), 2, tp).astype(bf16)
            kn = pad_rows(_heads_major(k, bs, ts, N_HEADS, HEAD_DIM), 2, NEW_PAD).astype(bf16)
            vn = pad_rows(_heads_major(v, bs, ts, N_HEADS, HEAD_DIM), 2, NEW_PAD).astype(bf16)
            ikn = pad_rows(ikw[:, :IDX_DIM].reshape(bs, ts, IDX_DIM), 1, NEW_PAD).astype(bf16)
            ikw3 = pad_rows(ikw.reshape(bs, ts, LANES), 1, tp)
            a = _dsa_sample(l, page_table, qh, iqh, ikw3, kn, vn, ikn, bias_s,
                            cache_idx_k, cache_k, cache_v)[:, :ts]
            prev = jnp.pad(state_conv[l], ((0, 0), (STATE_PAD, 0), (0, 0)))
            c, cst = _conv_module(cua.reshape(bs, ts, CONV_CH), cub.reshape(bs, ts, CONV_CH), prev,
                                  *conv_args(l), ts)
            mqp = pad_rows(mq.reshape(bs, ts, ATTN_W), 1, tp)
            m = _mem_attn(mqp, cache_mem_k[l].reshape(bs, MEM_LEN, ATTN_W),
                          cache_mem_v[l].reshape(bs, MEM_LEN, ATTN_W), tp)[:, :ts]
            return (a.reshape(bs * ts, ATTN_W), c.reshape(bs * ts, CONV_CH), cst,
                    m.reshape(bs * ts, ATTN_W), ())
        return mix

    xp = x_prompt.reshape(bp * sp, D_MODEL)
    xs = x_sample.reshape(bs * ts, D_MODEL)
    st_p, st_s = [], []
    for l in range(depth):
        xp, s_p = layer(xp, l, bp, sp, 512, prompt_mix(l))
        xs, s_s = layer(xs, l, bs, ts, bs * ts, sample_mix(l))
        st_p.append(s_p)
        st_s.append(s_s)

    outs_p = [jnp.stack([s[i] for s in st_p]) for i in range(6)]
    outs_s = [jnp.stack([s[i] for s in st_s]) for i in range(4)]
    return (xp.reshape(bp, sp, D_MODEL), xs.reshape(bs, ts, D_MODEL), *outs_p, *outs_s)
```

```python
import functools
import math

import numpy as np
import jax
import jax.numpy as jnp
from jax import lax
from jax.experimental import pallas as pl
from jax.experimental.pallas import tpu as pltpu

D_MODEL = 1024
N_HEADS = 8
HEAD_DIM = 64
ATTN_W = 512
IDX_HEADS = 8
IDX_DIM = 64
TOPK = 256
CONV_CH = 512
CONV_WIDTH = 31
MEM_LEN = 256
MEM_HEADS = 4
MEM_HEAD_DIM = 128
D_FF = 2816
N_BUCKETS = 32
MAX_DISTANCE = 128
LN_EPS = 1e-5
DEPTH = 2
ALPHA = (2 * DEPTH) ** 0.25
PAGE_SIZE = 128
PAST_LEN = 8192

LANES = 128
SUBLANES = 8
BF16_ROWS = 16
VMEM_LIMIT = 56 * 1024 * 1024

FF_CHUNK = 256
N_FF_CHUNKS = D_FF // FF_CHUNK
TQ = 256
STATE_ROWS = 32
STATE_PAD = STATE_ROWS - (CONV_WIDTH - 1)
SAMPLE_T_PAD = 8
NEW_PAD = 128
KV_GROUP_PAGES = 8
KV_GROUP = KV_GROUP_PAGES * PAGE_SIZE

IDX_SCALE = (IDX_DIM ** -0.5) * (IDX_HEADS ** -0.5)
LOG2E = math.log2(math.e)
QK_SCALE = (HEAD_DIM ** -0.5) * LOG2E
INT_MIN = -(2 ** 31)
KEY_NEG_INF = -2139095041
NEG_INF = float("-inf")

bf16 = jnp.bfloat16
f32 = jnp.float32


def _dot(a, b):
    return jnp.dot(a, b, preferred_element_type=f32)


def _dot_nt(a, b):
    return lax.dot_general(a, b, (((1,), (1,)), ((), ())), preferred_element_type=f32)


def _sigmoid(x):
    return 1.0 / (1.0 + jnp.exp(-x))


def _layer_norm(y, g, b):
    mu = jnp.mean(y, axis=-1, keepdims=True)
    d = y - mu
    var = jnp.mean(d * d, axis=-1, keepdims=True)
    return d * lax.rsqrt(var + LN_EPS) * g + b


def _key_to_float(key):
    bits = key ^ (lax.shift_right_arithmetic(key, 31) & 0x7FFFFFFF)
    return lax.bitcast_convert_type(bits, f32)


def _params(sem):
    return pltpu.CompilerParams(dimension_semantics=sem, vmem_limit_bytes=VMEM_LIMIT)


def _whole(shape):
    nd = len(shape)
    return pl.BlockSpec(shape, lambda *_: (0,) * nd)


def _ffn_ln_kernel(x_ref, wg_ref, wu_ref, wd_ref, g_ref, b_ref, o_ref, acc_ref):
    x = x_ref[...]
    xb = x.astype(bf16)
    acc_ref[...] = jnp.zeros_like(acc_ref)

    def chunk(c, carry):
        gate = _dot(xb, wg_ref[c])
        up = _dot(xb, wu_ref[c])
        h = (gate * _sigmoid(gate) * up).astype(bf16)
        acc_ref[...] += _dot(h, wd_ref[c])
        return carry

    lax.fori_loop(0, N_FF_CHUNKS, chunk, 0)
    y = ALPHA * x + 0.5 * acc_ref[...]
    o_ref[...] = _layer_norm(y, g_ref[...], b_ref[...])


def _ffn_ln(x, wg, wu, wd, g, b, tm):
    n = x.shape[0]
    return pl.pallas_call(
        _ffn_ln_kernel,
        grid=(n // tm,),
        in_specs=[pl.BlockSpec((tm, D_MODEL), lambda i: (i, 0)),
                  _whole(wg.shape), _whole(wu.shape), _whole(wd.shape),
                  _whole(g.shape), _whole(b.shape)],
        out_specs=pl.BlockSpec((tm, D_MODEL), lambda i: (i, 0)),
        out_shape=jax.ShapeDtypeStruct((n, D_MODEL), f32),
        scratch_shapes=[pltpu.VMEM((tm, D_MODEL), f32)],
        compiler_params=_params(("parallel",)),
        name="ffn_ln",
    )(x, wg, wu, wd, g, b)


def _proj_kernel(dtypes, x_ref, *refs):
    xb = x_ref[...].astype(bf16)
    o_refs = iter(refs[len(dtypes):])
    for w_ref, dts in zip(refs[:len(dtypes)], dtypes):
        r = _dot(xb, w_ref[...])
        for dt in dts:
            next(o_refs)[...] = r.astype(dt)


def _proj(x, ws, dtypes, tm):
    n = x.shape[0]
    outs = [(w.shape[1], dt) for w, dts in zip(ws, dtypes) for dt in dts]
    return pl.pallas_call(
        functools.partial(_proj_kernel, dtypes),
        grid=(n // tm,),
        in_specs=[pl.BlockSpec((tm, D_MODEL), lambda i: (i, 0))] + [_whole(w.shape) for w in ws],
        out_specs=[pl.BlockSpec((tm, c), lambda i: (i, 0)) for c, _ in outs],
        out_shape=[jax.ShapeDtypeStruct((n, c), dt) for c, dt in outs],
        compiler_params=_params(("parallel",)),
        name="in_proj",
    )(x, *ws)


def _conv_kernel(tt, n_tiles, a_ref, b_ref, prev_ref, dw_ref, db_ref, g_ref, bb_ref,
                 c_ref, st_ref, ext_ref):
    j = pl.program_id(1)

    @pl.when(j == 0)
    def _():
        ext_ref[0:STATE_ROWS, :] = prev_ref[0]

    ext_ref[STATE_ROWS:STATE_ROWS + tt, :] = a_ref[0] * _sigmoid(b_ref[0])
    acc = jnp.zeros((tt, CONV_CH), f32)
    for tap in range(CONV_WIDTH):
        acc = acc + ext_ref[STATE_PAD + tap:STATE_PAD + tap + tt, :] * dw_ref[tap:tap + 1, :]
    y = _layer_norm(acc + db_ref[...], g_ref[...], bb_ref[...])
    c_ref[0] = y * _sigmoid(y)
    tail = ext_ref[tt:tt + STATE_ROWS, :]
    st_ref[0] = tail
    if n_tiles > 1:
        ext_ref[0:STATE_ROWS, :] = tail


def _conv_module(a, b, prev, dw, db, g, bb, tt):
    bsz, t, _ = a.shape
    n_tiles = t // tt
    blk = pl.BlockSpec((1, tt, CONV_CH), lambda i, j: (i, j, 0))
    st = pl.BlockSpec((1, STATE_ROWS, CONV_CH), lambda i, j: (i, 0, 0))
    return pl.pallas_call(
        functools.partial(_conv_kernel, tt, n_tiles),
        grid=(bsz, n_tiles),
        in_specs=[blk, blk, st, _whole(dw.shape), _whole(db.shape), _whole(g.shape), _whole(bb.shape)],
        out_specs=[blk, st],
        out_shape=[jax.ShapeDtypeStruct((bsz, t, CONV_CH), f32),
                   jax.ShapeDtypeStruct((bsz, STATE_ROWS, CONV_CH), f32)],
        scratch_shapes=[pltpu.VMEM((STATE_ROWS + tt, CONV_CH), f32)],
        compiler_params=_params(("parallel", "arbitrary")),
        name="conv_module",
    )(a, b, prev, dw, db, g, bb)


def _mem_attn_kernel(q_ref, mk_ref, mv_ref, o_ref):
    scale = MEM_HEAD_DIM ** -0.5
    for h in range(MEM_HEADS):
        sl = slice(h * MEM_HEAD_DIM, (h + 1) * MEM_HEAD_DIM)
        qh = (q_ref[0, :, sl] * scale).astype(bf16)
        kh = mk_ref[0, :, sl].astype(bf16)
        vh = mv_ref[0, :, sl].astype(bf16)
        lg = _dot_nt(qh, kh)
        mx = jnp.max(lg, axis=-1, keepdims=True)
        p = jnp.exp(lg - mx)
        den = jnp.sum(p, axis=-1, keepdims=True)
        o_ref[0, :, sl] = _dot((p / den).astype(bf16), vh)


def _mem_attn(q, mk, mv, tq):
    bsz, t, w = q.shape
    qblk = pl.BlockSpec((1, tq, w), lambda i, j: (i, j, 0))
    mblk = pl.BlockSpec((1, MEM_LEN, w), lambda i, j: (i, 0, 0))
    return pl.pallas_call(
        _mem_attn_kernel,
        grid=(bsz, t // tq),
        in_specs=[qblk, mblk, mblk],
        out_specs=qblk,
        out_shape=jax.ShapeDtypeStruct((bsz, t, w), f32),
        compiler_params=_params(("parallel", "parallel")),
        name="mem_attn",
    )(q, mk, mv)


def _merge_ln_kernel(x_ref, a_ref, c_ref, m_ref, g0_ref, g1_ref, g2_ref,
                     wa_ref, wc_ref, wm_ref, wo_ref, g_ref, b_ref, o_ref):
    y = _sigmoid(g0_ref[...]) * _dot(a_ref[...].astype(bf16), wa_ref[...])
    y = y + _sigmoid(g1_ref[...]) * _dot(c_ref[...].astype(bf16), wc_ref[...])
    y = y + _sigmoid(g2_ref[...]) * _dot(m_ref[...].astype(bf16), wm_ref[...])
    z = _dot(y.astype(bf16), wo_ref[...])
    o_ref[...] = _layer_norm(ALPHA * x_ref[...] + z, g_ref[...], b_ref[...])


def _merge_ln(x, a, c, m, g0, g1, g2, wa, wc, wm, wo, g, b, tm):
    n = x.shape[0]
    wide = pl.BlockSpec((tm, D_MODEL), lambda i: (i, 0))
    half = pl.BlockSpec((tm, ATTN_W), lambda i: (i, 0))
    return pl.pallas_call(
        _merge_ln_kernel,
        grid=(n // tm,),
        in_specs=[wide, half, half, half, wide, wide, wide,
                  _whole(wa.shape), _whole(wc.shape), _whole(wm.shape), _whole(wo.shape),
                  _whole(g.shape), _whole(b.shape)],
        out_specs=wide,
        out_shape=jax.ShapeDtypeStruct((n, D_MODEL), f32),
        compiler_params=_params(("parallel",)),
        name="merge_ln",
    )(x, a, c, m, g0, g1, g2, wa, wc, wm, wo, g, b)


def _rel_bucket_np(n):
    n = np.maximum(n, 0)
    exact = N_BUCKETS // 2
    nf = np.maximum(n, 1).astype(np.float64)
    large = exact + (np.log(nf / exact) / math.log(MAX_DISTANCE / exact) * (N_BUCKETS - exact)).astype(np.int64)
    large = np.minimum(large, N_BUCKETS - 1)
    return np.where(n < exact, n, large).astype(np.int32)


def _bias_kernel(rb_ref, bkt_ref, o_ref):
    n_tiles = bkt_ref.shape[0]
    for t in range(n_tiles):
        bkt = bkt_ref[t]
        for h in range(N_HEADS):
            far = rb_ref[N_BUCKETS - 1, h]
            acc = jnp.zeros(bkt.shape, f32)
            for bucket in range(N_BUCKETS - 1):
                acc = jnp.where(bkt == bucket, (rb_ref[bucket, h] - far) * LOG2E, acc)
            o_ref[t, h] = acc


def _bias_tiles(rel_bias, buckets):
    n_tiles, r, c = buckets.shape
    return pl.pallas_call(
        _bias_kernel,
        in_specs=[pl.BlockSpec(memory_space=pltpu.SMEM), _whole(buckets.shape)],
        out_specs=_whole((n_tiles, N_HEADS, r, c)),
        out_shape=jax.ShapeDtypeStruct((n_tiles, N_HEADS, r, c), f32),
        grid=(1,),
        name="rel_bias_tiles",
    )(rel_bias, buckets)


ACC_ROWS = HEAD_DIM + BF16_ROWS


def _dsa_prompt_kernel(q_ref, iq_ref, ikw_ref, k_ref, vt_ref, ik2_ref, bias_ref, o_ref,
                       s_ref, qp_ref, iqp_ref, lg_ref, m_ref, acc_ref):
    j = pl.program_id(1)
    n_chunks = j + 1
    kpos0 = lax.broadcasted_iota(jnp.int32, (TQ, TQ), 0)
    qpos0 = lax.broadcasted_iota(jnp.int32, (TQ, TQ), 1)

    def chunk_rows(c):
        return pl.ds(pl.multiple_of(c * TQ, TQ), TQ)

    lane = lax.broadcasted_iota(jnp.int32, (TQ, LANES), 1)
    for h in range(N_HEADS):
        pair = slice((h // 2) * LANES, (h // 2 + 1) * LANES)
        mine = (lane < HEAD_DIM) if h % 2 == 0 else (lane >= HEAD_DIM)
        qp_ref[h] = jnp.where(mine, q_ref[0, :, pair] * QK_SCALE, 0.0).astype(bf16)
        iqp_ref[h] = jnp.where(mine, iq_ref[0, :, pair], jnp.zeros((), bf16))

    w = ikw_ref[0].T[IDX_DIM:IDX_DIM + IDX_HEADS, :] * IDX_SCALE

    def score_chunk(c, carry):
        ikc = ik2_ref[0, chunk_rows(c), :]
        s = jnp.zeros((TQ, TQ), f32)
        for h in range(IDX_HEADS):
            d = _dot_nt(ikc, iqp_ref[h])
            s = s + jnp.maximum(d, 0.0) * w[h:h + 1, :]
        s_ref[chunk_rows(c), :] = jnp.where(kpos0 + (c - j) * TQ <= qpos0, s, NEG_INF)
        return carry

    lax.fori_loop(0, n_chunks, score_chunk, 0)

    def column_count(hit_fn):
        def body(c, cnt):
            hit = jnp.where(hit_fn(c, s_ref[chunk_rows(c), :]), 1, 0)
            return cnt + jnp.sum(hit.reshape(TQ // SUBLANES, SUBLANES, TQ), axis=0)

        cnt = lax.fori_loop(0, n_chunks, body, jnp.zeros((SUBLANES, TQ), jnp.int32))
        return jnp.sum(cnt, axis=0, keepdims=True)

    def search(i, carry):
        t_key, t_cnt = carry
        cand = t_key + lax.shift_left(jnp.int32(1), 31 - i)
        thr_c = _key_to_float(cand)
        cnt = column_count(lambda c, blk: blk >= thr_c)
        ok = cnt >= TOPK
        return jnp.where(ok, cand, t_key), jnp.where(ok, cnt, t_cnt)

    t_key, t_cnt = lax.fori_loop(
        0, 32, search,
        (jnp.full((1, TQ), INT_MIN, jnp.int32), jnp.full((1, TQ), 1 << 30, jnp.int32)))
    has_k = t_key > KEY_NEG_INF
    thr = _key_to_float(jnp.maximum(t_key, KEY_NEG_INF + 1))

    tied = jnp.logical_and(has_k, t_cnt > TOPK)

    @pl.when(jnp.max(jnp.where(tied, 1, 0)) > 0)
    def _():
        def count_eq_below(pos):
            return column_count(
                lambda c, blk: jnp.logical_and(blk == thr, kpos0 + c * TQ < pos))

        n_eq = count_eq_below(jnp.full((1, TQ), 1 << 30, jnp.int32))
        need = jnp.where(tied, TOPK - (t_cnt - n_eq), 1 << 30)

        def psearch(i, pos):
            cand = pos + lax.shift_left(jnp.int32(1), 12 - i)
            return jnp.where(count_eq_below(cand) <= need - 1, cand, pos)

        pmax = lax.fori_loop(0, 13, psearch, jnp.zeros((1, TQ), jnp.int32))

        def drop(c, carry):
            s = s_ref[chunk_rows(c), :]
            s_ref[chunk_rows(c), :] = jnp.where(
                jnp.logical_and(s == thr, kpos0 + c * TQ > pmax), NEG_INF, s)
            return carry

        lax.fori_loop(0, n_chunks, drop, 0)

    def to_mask(c, carry):
        s_ref[chunk_rows(c), :] = jnp.where(s_ref[chunk_rows(c), :] >= thr, 0.0, NEG_INF)
        return carry

    lax.fori_loop(0, n_chunks, to_mask, 0)

    m_ref[...] = jnp.full(m_ref.shape, NEG_INF, f32)
    acc_ref[...] = jnp.zeros(acc_ref.shape, f32)
    ones = jnp.ones((BF16_ROWS, TQ), bf16)

    def attend_chunk(c, bias_tile):
        rows = chunk_rows(c)
        mask = s_ref[rows, :]
        stats = []
        for h in range(N_HEADS):
            pair = (h // 2) * LANES
            lg = _dot_nt(k_ref[0, rows, pair:pair + LANES], qp_ref[h]) + mask
            if bias_tile is not None:
                lg = lg + bias_ref[bias_tile, h]
            lg_ref[h] = lg
            m_old = m_ref[h]
            m_new = jnp.maximum(m_old, jnp.max(lg, axis=0, keepdims=True))
            m_safe = jnp.where(m_new == NEG_INF, 0.0, m_new)
            m_ref[h] = m_new
            stats.append((m_safe, jnp.exp2(m_old - m_safe)))
        for h in range(N_HEADS):
            m_safe, alpha = stats[h]
            p = jnp.exp2(lg_ref[h] - m_safe).astype(bf16)
            va = jnp.concatenate([vt_ref[0, h * HEAD_DIM:(h + 1) * HEAD_DIM, rows], ones], axis=0)
            acc_ref[h] = alpha * acc_ref[h] + _dot(va, p)

    def far_chunk(c, carry):
        attend_chunk(c, None)
        return carry

    lax.fori_loop(0, jnp.maximum(j - 1, 0), far_chunk, 0)

    @pl.when(j >= 1)
    def _():
        attend_chunk(j - 1, 1)

    attend_chunk(j, 0)

    for pair in range(N_HEADS // 2):
        even, odd = acc_ref[2 * pair], acc_ref[2 * pair + 1]
        out_t = jnp.concatenate([even[0:HEAD_DIM] / even[HEAD_DIM:HEAD_DIM + 1],
                                 odd[0:HEAD_DIM] / odd[HEAD_DIM:HEAD_DIM + 1]], axis=0)
        o_ref[0, :, pair * LANES:(pair + 1) * LANES] = out_t.T


def _dsa_prompt(q, iqb, ikw, kb, vt, ik2, bias):
    bsz, s, _ = q.shape
    once = pl.Buffered(1)
    qblk = lambda width: pl.BlockSpec((1, TQ, width), lambda b, j: (b, j, 0))
    return pl.pallas_call(
        _dsa_prompt_kernel,
        grid=(bsz, s // TQ),
        in_specs=[qblk(ATTN_W), qblk(IDX_HEADS * IDX_DIM), qblk(LANES),
                  pl.BlockSpec((1, s, ATTN_W), lambda b, j: (b, 0, 0), pipeline_mode=once),
                  pl.BlockSpec((1, ATTN_W, s), lambda b, j: (b, 0, 0), pipeline_mode=once),
                  pl.BlockSpec((1, s, LANES), lambda b, j: (b, 0, 0), pipeline_mode=once),
                  _whole(bias.shape)],
        out_specs=qblk(ATTN_W),
        out_shape=jax.ShapeDtypeStruct((bsz, s, ATTN_W), f32),
        scratch_shapes=[pltpu.VMEM((s, TQ), f32),
                        pltpu.VMEM((N_HEADS, TQ, LANES), bf16),
                        pltpu.VMEM((IDX_HEADS, TQ, LANES), bf16),
                        pltpu.VMEM((N_HEADS, TQ, TQ), f32),
                        pltpu.VMEM((N_HEADS, 1, TQ), f32),
                        pltpu.VMEM((N_HEADS, ACC_ROWS, TQ), f32)],
        compiler_params=_params(("parallel", "arbitrary")),
        name="dsa_prompt",
    )(q, iqb, ikw, kb, vt, ik2, bias)


N_PAGES = PAST_LEN // PAGE_SIZE
N_KV_GROUPS = N_PAGES // KV_GROUP_PAGES
S_KEYS = PAST_LEN + NEW_PAD


def _softmax_max(lg, m_ref, h):
    m_old = m_ref[h]
    m_new = jnp.maximum(m_old, jnp.max(lg, axis=-1, keepdims=True))
    m_safe = jnp.where(m_new == NEG_INF, 0.0, m_new)
    m_ref[h] = m_new
    return m_safe, jnp.exp2(m_old - m_safe)


def _softmax_acc(lg, m_safe, alpha, l_ref, acc_ref, h, vt):
    p = jnp.exp2(lg - m_safe[:, 0:1])
    l_ref[h] = alpha * l_ref[h] + jnp.sum(p, axis=-1, keepdims=True)
    acc_ref[h] = alpha[:, 0:HEAD_DIM] * acc_ref[h] + _dot_nt(p.astype(bf16), vt)


def _dsa_sample_kernel(layer, pt_ref, q_ref, iq_ref, ikw_ref, knt_ref, vnt_ref, iknt_ref,
                       bias_ref, cik_ref, ck_ref, cv_ref, o_ref,
                       ikbuf, kbuf, vbuf, s_ref, m_ref, l_ref, acc_ref, sem_ik, sem_kv):
    b = pl.program_id(0)
    tp = SAMPLE_T_PAD

    def page_lanes(i):
        return pl.ds(pl.multiple_of(i * PAGE_SIZE, PAGE_SIZE), PAGE_SIZE)

    def ik_copy(p, phys):
        return pltpu.make_async_copy(cik_ref.at[layer, phys], ikbuf.at[:, page_lanes(p)], sem_ik.at[0])

    def kv_copies(slot, i, phys):
        return (pltpu.make_async_copy(ck_ref.at[layer, phys], kbuf.at[slot, :, :, page_lanes(i)], sem_kv.at[slot]),
                pltpu.make_async_copy(cv_ref.at[layer, phys], vbuf.at[slot, :, :, page_lanes(i)], sem_kv.at[slot]))

    def start_group(g, slot):
        def body(i, carry):
            for cp in kv_copies(slot, i, pt_ref[b, g * KV_GROUP_PAGES + i]):
                cp.start()
            return carry
        lax.fori_loop(0, KV_GROUP_PAGES, body, 0)

    def wait_group(slot):
        def body(i, carry):
            for cp in kv_copies(slot, i, 0):
                cp.wait()
            return carry
        lax.fori_loop(0, KV_GROUP_PAGES, body, 0)

    def start_ik(p, carry):
        ik_copy(p, pt_ref[b, p]).start()
        return carry

    def wait_ik(p, carry):
        ik_copy(p, 0).wait()
        return carry

    lax.fori_loop(0, N_PAGES, start_ik, 0)
    start_group(0, 0)
    lax.fori_loop(0, N_PAGES, wait_ik, 0)

    ikw = ikw_ref[0]
    row = lax.broadcasted_iota(jnp.int32, (tp, NEW_PAD), 0)
    col = lax.broadcasted_iota(jnp.int32, (tp, NEW_PAD), 1)
    for g in range(N_KV_GROUPS):
        ikc = ikbuf[:, g * KV_GROUP:(g + 1) * KV_GROUP].astype(bf16)
        s = jnp.zeros((tp, KV_GROUP), f32)
        for h in range(IDX_HEADS):
            d = _dot(iq_ref[0, h], ikc)
            s = s + jnp.maximum(d, 0.0) * (ikw[:, IDX_DIM + h:IDX_DIM + h + 1] * IDX_SCALE)
        s_ref[:, g * KV_GROUP:(g + 1) * KV_GROUP] = s
    s = jnp.zeros((tp, NEW_PAD), f32)
    for h in range(IDX_HEADS):
        d = _dot(iq_ref[0, h], iknt_ref[0])
        s = s + jnp.maximum(d, 0.0) * (ikw[:, IDX_DIM + h:IDX_DIM + h + 1] * IDX_SCALE)
    s_ref[:, PAST_LEN:] = jnp.where(col <= row, s, NEG_INF)

    def count_ge(thr):
        hit = jnp.where(s_ref[...] >= thr, 1, 0)
        return jnp.sum(hit, axis=-1, keepdims=True)

    def search(i, carry):
        t_key, t_cnt = carry
        cand = t_key + lax.shift_left(jnp.int32(1), 31 - i)
        cnt = count_ge(_key_to_float(cand))
        ok = cnt >= TOPK
        return jnp.where(ok, cand, t_key), jnp.where(ok, cnt, t_cnt)

    t_key, t_cnt = lax.fori_loop(
        0, 32, search,
        (jnp.full((tp, 1), INT_MIN, jnp.int32), jnp.full((tp, 1), 1 << 30, jnp.int32)))
    has_k = t_key > KEY_NEG_INF
    thr = _key_to_float(jnp.maximum(t_key, KEY_NEG_INF + 1))
    tied = jnp.logical_and(has_k, t_cnt > TOPK)

    @pl.when(jnp.max(jnp.where(tied, 1, 0)) > 0)
    def _():
        kpos = lax.broadcasted_iota(jnp.int32, (tp, S_KEYS), 1)

        def count_eq_below(pos):
            hit = jnp.where(jnp.logical_and(s_ref[...] == thr, kpos < pos), 1, 0)
            return jnp.sum(hit, axis=-1, keepdims=True)

        n_eq = count_eq_below(jnp.full((tp, 1), 1 << 30, jnp.int32))
        need = jnp.where(tied, TOPK - (t_cnt - n_eq), 1 << 30)

        def psearch(i, pos):
            cand = pos + lax.shift_left(jnp.int32(1), 13 - i)
            return jnp.where(count_eq_below(cand) <= need - 1, cand, pos)

        pmax = lax.fori_loop(0, 14, psearch, jnp.zeros((tp, 1), jnp.int32))
        s = s_ref[...]
        s_ref[...] = jnp.where(jnp.logical_and(s == thr, kpos > pmax), NEG_INF, s)

    m_ref[...] = jnp.full(m_ref.shape, NEG_INF, f32)
    l_ref[...] = jnp.zeros(l_ref.shape, f32)
    acc_ref[...] = jnp.zeros(acc_ref.shape, f32)

    def group(g, carry):
        slot = g % 2

        @pl.when(g + 1 < N_KV_GROUPS)
        def _():
            start_group(g + 1, 1 - slot)

        wait_group(slot)
        ks = pl.multiple_of(g * KV_GROUP, KV_GROUP)
        sel = s_ref[:, pl.ds(ks, KV_GROUP)] >= thr
        near = jnp.where(g == N_KV_GROUPS - 1, 1.0, 0.0)
        logits, stats = [], []
        for h in range(N_HEADS):
            qh = (q_ref[0, h] * QK_SCALE).astype(bf16)
            lg = _dot(qh, kbuf[slot, h].astype(bf16)) + near * bias_ref[0, h]
            lg = jnp.where(sel, lg, NEG_INF)
            logits.append(lg)
            stats.append(_softmax_max(lg, m_ref, h))
        for h in range(N_HEADS):
            _softmax_acc(logits[h], *stats[h], l_ref, acc_ref, h, vbuf[slot, h].astype(bf16))
        return carry

    lax.fori_loop(0, N_KV_GROUPS, group, 0)

    sel = s_ref[:, PAST_LEN:] >= thr
    logits, stats = [], []
    for h in range(N_HEADS):
        qh = (q_ref[0, h] * QK_SCALE).astype(bf16)
        lg = _dot(qh, knt_ref[0, h]) + bias_ref[1, h, :, 0:NEW_PAD]
        lg = jnp.where(sel, lg, NEG_INF)
        logits.append(lg)
        stats.append(_softmax_max(lg, m_ref, h))
    for h in range(N_HEADS):
        _softmax_acc(logits[h], *stats[h], l_ref, acc_ref, h, vnt_ref[0, h])
        o_ref[0, :, h * HEAD_DIM:(h + 1) * HEAD_DIM] = acc_ref[h] / l_ref[h][:, 0:HEAD_DIM]


def _dsa_sample(layer, page_table, qh, iqh, ikw, knt, vnt, iknt, bias, cik_t, ck_t, cv_t):
    bsz = qh.shape[0]
    tp = SAMPLE_T_PAD
    any_spec = pl.BlockSpec(memory_space=pl.ANY)
    grid_spec = pltpu.PrefetchScalarGridSpec(
        num_scalar_prefetch=1,
        grid=(bsz,),
        in_specs=[pl.BlockSpec((1, N_HEADS, tp, HEAD_DIM), lambda b, pt: (b, 0, 0, 0)),
                  pl.BlockSpec((1, IDX_HEADS, tp, IDX_DIM), lambda b, pt: (b, 0, 0, 0)),
                  pl.BlockSpec((1, tp, LANES), lambda b, pt: (b, 0, 0)),
                  pl.BlockSpec((1, N_HEADS, HEAD_DIM, NEW_PAD), lambda b, pt: (b, 0, 0, 0)),
                  pl.BlockSpec((1, N_HEADS, HEAD_DIM, NEW_PAD), lambda b, pt: (b, 0, 0, 0)),
                  pl.BlockSpec((1, IDX_DIM, NEW_PAD), lambda b, pt: (b, 0, 0)),
                  pl.BlockSpec(bias.shape, lambda b, pt: (0, 0, 0, 0)),
                  any_spec, any_spec, any_spec],
        out_specs=pl.BlockSpec((1, tp, ATTN_W), lambda b, pt: (b, 0, 0)),
        scratch_shapes=[pltpu.VMEM((IDX_DIM, PAST_LEN), f32),
                        pltpu.VMEM((2, N_HEADS, HEAD_DIM, KV_GROUP), f32),
                        pltpu.VMEM((2, N_HEADS, HEAD_DIM, KV_GROUP), f32),
                        pltpu.VMEM((tp, S_KEYS), f32),
                        pltpu.VMEM((N_HEADS, tp, LANES), f32),
                        pltpu.VMEM((N_HEADS, tp, LANES), f32),
                        pltpu.VMEM((N_HEADS, tp, HEAD_DIM), f32),
                        pltpu.SemaphoreType.DMA((1,)),
                        pltpu.SemaphoreType.DMA((2,))],
    )
    return pl.pallas_call(
        functools.partial(_dsa_sample_kernel, layer),
        grid_spec=grid_spec,
        out_shape=jax.ShapeDtypeStruct((bsz, tp, ATTN_W), f32),
        compiler_params=_params(("arbitrary",)),
        name="dsa_sample",
    )(page_table, qh, iqh, ikw, knt, vnt, iknt, bias, cik_t, ck_t, cv_t)


def _prompt_buckets():
    k = np.arange(TQ)[:, None]
    q = np.arange(TQ)[None, :]
    return np.stack([_rel_bucket_np(q - k), _rel_bucket_np(TQ + q - k)])


def _sample_buckets():
    t = np.arange(SAMPLE_T_PAD)[:, None]
    k = np.arange(KV_GROUP)[None, :]
    last = _rel_bucket_np(PAST_LEN + t - ((N_KV_GROUPS - 1) * KV_GROUP + k))
    new = np.full((SAMPLE_T_PAD, KV_GROUP), N_BUCKETS - 1, np.int32)
    new[:, :NEW_PAD] = _rel_bucket_np(t - np.arange(NEW_PAD)[None, :])
    return np.stack([last, new])


def _heads_major(x, bsz, t, heads, dim):
    return x.reshape(bsz, t, heads, dim).transpose(0, 2, 1, 3)


def kernel(x_prompt, x_sample, mem_prompt, cache_k, cache_v, cache_idx_k, state_conv, cache_mem_k, cache_mem_v, page_table, ln_g, ln_b, w_ff_gate, w_ff_up, w_ff_down, w_in, conv_dw, conv_db, conv_ln_g, conv_ln_b, w_mem_kv, w_br_attn, w_br_conv, w_br_mem, w_o, rel_bias):
    bp, sp, _ = x_prompt.shape
    bs, ts, _ = x_sample.shape
    depth = w_in.shape[0]

    wg = w_ff_gate.astype(bf16).reshape(depth, 2, D_MODEL, N_FF_CHUNKS, FF_CHUNK).transpose(0, 1, 3, 2, 4)
    wu = w_ff_up.astype(bf16).reshape(depth, 2, D_MODEL, N_FF_CHUNKS, FF_CHUNK).transpose(0, 1, 3, 2, 4)
    wd = w_ff_down.astype(bf16).reshape(depth, 2, N_FF_CHUNKS, FF_CHUNK, D_MODEL)
    w_inb = w_in.astype(bf16)
    edges = np.cumsum([0, ATTN_W, ATTN_W, ATTN_W, IDX_HEADS * IDX_DIM, IDX_DIM + IDX_HEADS,
                       CONV_CH, CONV_CH, MEM_HEADS * MEM_HEAD_DIM, D_MODEL, D_MODEL, D_MODEL])

    def in_weights(l):
        ws = [w_inb[l, :, edges[i]:edges[i + 1]] for i in range(len(edges) - 1)]
        w_ik = ws[4][:, :IDX_DIM]
        ws[4] = jnp.pad(ws[4], ((0, 0), (0, LANES - ws[4].shape[1])))
        ws.append(jnp.concatenate([w_ik, w_ik], axis=1))
        return ws

    in_dtypes = ((f32,), (f32, bf16), (f32,), (bf16,), (f32,), (f32,), (f32,), (f32,),
                 (f32,), (f32,), (f32,), (bf16,))

    w_memb = w_mem_kv.astype(bf16)
    wab, wcb, wmb, wob = (w.astype(bf16) for w in (w_br_attn, w_br_conv, w_br_mem, w_o))
    lng = ln_g.reshape(depth, 3, 1, D_MODEL)
    lnb = ln_b.reshape(depth, 3, 1, D_MODEL)

    bias_p = _bias_tiles(rel_bias, jnp.asarray(_prompt_buckets()))
    bias_s = _bias_tiles(rel_bias, jnp.asarray(_sample_buckets()))

    ck_t = cache_k.transpose(0, 1, 3, 4, 2)
    cv_t = cache_v.transpose(0, 1, 3, 4, 2)
    cik_t = cache_idx_k.transpose(0, 1, 3, 2)

    def layer(x, l, bsz, t, tm, mix):
        x = _ffn_ln(x, wg[l, 0], wu[l, 0], wd[l, 0], lng[l, 0], lnb[l, 0], tm)
        q, k, kb, v, iqb, ikw, cua, cub, mq, g0, g1, g2, ik2 = _proj(
            x, in_weights(l), in_dtypes, min(tm, 256))
        a, c, cst, m, extra = mix(q, k, kb, v, iqb, ikw, ik2, cua, cub, mq)
        x = _merge_ln(x, a, c, m, g0, g1, g2, wab[l], wcb[l], wmb[l], wob[l],
                      lng[l, 1], lnb[l, 1], min(tm, 256))
        x = _ffn_ln(x, wg[l, 1], wu[l, 1], wd[l, 1], lng[l, 2], lnb[l, 2], tm)
        state = (k.reshape(bsz, t, N_HEADS, HEAD_DIM), v.reshape(bsz, t, N_HEADS, HEAD_DIM),
                 ikw[:, :IDX_DIM].reshape(bsz, t, IDX_DIM), cst[:, STATE_PAD:]) + extra
        return x, state

    def conv_args(l):
        return (conv_dw[l], conv_db[l].reshape(1, CONV_CH), conv_ln_g[l].reshape(1, CONV_CH),
                conv_ln_b[l].reshape(1, CONV_CH))

    def prompt_mix(l):
        def mix(q, k, kb, v, iqb, ikw, ik2, cua, cub, mq):
            vt = v.reshape(bp, sp, ATTN_W).transpose(0, 2, 1).astype(bf16)
            a = _dsa_prompt(q.reshape(bp, sp, ATTN_W), iqb.reshape(bp, sp, IDX_HEADS * IDX_DIM),
                            ikw.reshape(bp, sp, LANES), kb.reshape(bp, sp, ATTN_W), vt,
                            ik2.reshape(bp, sp, LANES), bias_p)
            prev = jnp.zeros((bp, STATE_ROWS, CONV_CH), f32)
            c, cst = _conv_module(cua.reshape(bp, sp, CONV_CH), cub.reshape(bp, sp, CONV_CH), prev,
                                  *conv_args(l), 512)
            mem = mem_prompt.reshape(bp * MEM_LEN, D_MODEL)
            mk, mv = _proj(mem, [w_memb[l, :, :ATTN_W], w_memb[l, :, ATTN_W:]], ((f32,), (f32,)), 256)
            m = _mem_attn(mq.reshape(bp, sp, ATTN_W), mk.reshape(bp, MEM_LEN, ATTN_W),
                          mv.reshape(bp, MEM_LEN, ATTN_W), 512)
            extra = (mk.reshape(bp, MEM_LEN, MEM_HEADS, MEM_HEAD_DIM),
                     mv.reshape(bp, MEM_LEN, MEM_HEADS, MEM_HEAD_DIM))
            return (a.reshape(bp * sp, ATTN_W), c.reshape(bp * sp, CONV_CH), cst,
                    m.reshape(bp * sp, ATTN_W), extra)
        return mix

    def pad_to(x, axis, size):
        pad = [(0, 0)] * x.ndim
        pad[axis] = (0, size - x.shape[axis])
        return jnp.pad(x, pad)

    def sample_mix(l):
        def mix(q, k, kb, v, iqb, ikw, ik2, cua, cub, mq):
            tp = SAMPLE_T_PAD
            qh = pad_to(_heads_major(q, bs, ts, N_HEADS, HEAD_DIM), 2, tp)
            iqh = pad_to(_heads_major(iqb, bs, ts, IDX_HEADS, IDX_DIM), 2, tp)
            knt = pad_to(k.reshape(bs, ts, N_HEADS, HEAD_DIM).transpose(0, 2, 3, 1), 3, NEW_PAD).astype(bf16)
            vnt = pad_to(v.reshape(bs, ts, N_HEADS, HEAD_DIM).transpose(0, 2, 3, 1), 3, NEW_PAD).astype(bf16)
            iknt = pad_to(ikw[:, :IDX_DIM].reshape(bs, ts, IDX_DIM).transpose(0, 2, 1), 2, NEW_PAD).astype(bf16)
            ikw3 = pad_to(ikw.reshape(bs, ts, LANES), 1, tp)
            a = _dsa_sample(l, page_table, qh, iqh, ikw3, knt, vnt, iknt, bias_s,
                            cik_t, ck_t, cv_t)[:, :ts]
            prev = jnp.pad(state_conv[l], ((0, 0), (STATE_PAD, 0), (0, 0)))
            c, cst = _conv_module(cua.reshape(bs, ts, CONV_CH), cub.reshape(bs, ts, CONV_CH), prev,
                                  *conv_args(l), ts)
            mqp = pad_to(mq.reshape(bs, ts, ATTN_W), 1, tp)
            m = _mem_attn(mqp, cache_mem_k[l].reshape(bs, MEM_LEN, ATTN_W),
                          cache_mem_v[l].reshape(bs, MEM_LEN, ATTN_W), tp)[:, :ts]
            return (a.reshape(bs * ts, ATTN_W), c.reshape(bs * ts, CONV_CH), cst,
                    m.reshape(bs * ts, ATTN_W), ())
        return mix

    xp = x_prompt.reshape(bp * sp, D_MODEL)
    xs = x_sample.reshape(bs * ts, D_MODEL)
    st_p, st_s = [], []
    for l in range(depth):
        xp, s_p = layer(xp, l, bp, sp, 512, prompt_mix(l))
        xs, s_s = layer(xs, l, bs, ts, bs * ts, sample_mix(l))
        st_p.append(s_p)
        st_s.append(s_s)

    outs_p = [jnp.stack([s[i] for s in st_p]) for i in range(6)]
    outs_s = [jnp.stack([s[i] for s in st_s]) for i in range(4)]
    return (xp.reshape(bp, sp, D_MODEL), xs.reshape(bs, ts, D_MODEL), *outs_p, *outs_s)
```

```python
import functools
import math

import numpy as np
import jax
import jax.numpy as jnp
from jax import lax
from jax.experimental import pallas as pl
from jax.experimental.pallas import tpu as pltpu

D_MODEL = 1024
N_HEADS = 8
HEAD_DIM = 64
ATTN_W = 512
IDX_HEADS = 8
IDX_DIM = 64
TOPK = 256
CONV_CH = 512
CONV_WIDTH = 31
MEM_LEN = 256
MEM_HEADS = 4
MEM_HEAD_DIM = 128
D_FF = 2816
N_BUCKETS = 32
MAX_DISTANCE = 128
LN_EPS = 1e-5
DEPTH = 2
ALPHA = (2 * DEPTH) ** 0.25
PAGE_SIZE = 128
PAST_LEN = 8192

LANES = 128
SUBLANES = 8
BF16_ROWS = 16
VMEM_LIMIT = 56 * 1024 * 1024

FF_CHUNK = 256
N_FF_CHUNKS = D_FF // FF_CHUNK
TQ = 256
STATE_ROWS = 32
STATE_PAD = STATE_ROWS - (CONV_WIDTH - 1)
SAMPLE_T_PAD = 8
NEW_PAD = 128
KV_GROUP_PAGES = 8
KV_GROUP = KV_GROUP_PAGES * PAGE_SIZE

IDX_SCALE = (IDX_DIM ** -0.5) * (IDX_HEADS ** -0.5)
LOG2E = math.log2(math.e)
QK_SCALE = (HEAD_DIM ** -0.5) * LOG2E
INT_MIN = -(2 ** 31)
HALF16 = 2 ** 15
KEY_NEG_INF = -2139095041
NEG_INF = float("-inf")

bf16 = jnp.bfloat16
f32 = jnp.float32


def _dot(a, b):
    return jnp.dot(a, b, preferred_element_type=f32)


def _dot_nt(a, b):
    return lax.dot_general(a, b, (((1,), (1,)), ((), ())), preferred_element_type=f32)


def _sigmoid(x):
    return 1.0 / (1.0 + jnp.exp(-x))


def _layer_norm(y, g, b):
    mu = jnp.mean(y, axis=-1, keepdims=True)
    d = y - mu
    var = jnp.mean(d * d, axis=-1, keepdims=True)
    return d * lax.rsqrt(var + LN_EPS) * g + b


def _key_to_float(key):
    bits = key ^ (lax.shift_right_arithmetic(key, 31) & 0x7FFFFFFF)
    return lax.bitcast_convert_type(bits, f32)


def _params(sem):
    return pltpu.CompilerParams(dimension_semantics=sem, vmem_limit_bytes=VMEM_LIMIT)


def _whole(shape):
    nd = len(shape)
    return pl.BlockSpec(shape, lambda *_: (0,) * nd)


LN_ROWS = 128


def _ffn_ln_kernel(x_ref, wgu_ref, wd_ref, g_ref, b_ref, o_ref, xb_ref, acc_ref):
    xb_ref[...] = x_ref[...].astype(bf16)
    acc_ref[...] = jnp.zeros_like(acc_ref)

    def chunk(c, carry):
        gu = _dot(xb_ref[...], wgu_ref[c])
        gate, up = gu[:, :FF_CHUNK], gu[:, FF_CHUNK:]
        h = (gate * _sigmoid(gate) * up).astype(bf16)
        acc_ref[...] += _dot(h, wd_ref[c])
        return carry

    lax.fori_loop(0, N_FF_CHUNKS, chunk, 0)

    def post_norm(r, carry):
        rows = pl.ds(pl.multiple_of(r * LN_ROWS, LN_ROWS), LN_ROWS)
        y = ALPHA * x_ref[rows, :] + 0.5 * acc_ref[rows, :]
        o_ref[rows, :] = _layer_norm(y, g_ref[...], b_ref[...])
        return carry

    lax.fori_loop(0, x_ref.shape[0] // LN_ROWS, post_norm, 0)


def _ffn_ln(x, wgu, wd, g, b, tm):
    n = x.shape[0]
    return pl.pallas_call(
        _ffn_ln_kernel,
        grid=(n // tm,),
        in_specs=[pl.BlockSpec((tm, D_MODEL), lambda i: (i, 0)),
                  _whole(wgu.shape), _whole(wd.shape), _whole(g.shape), _whole(b.shape)],
        out_specs=pl.BlockSpec((tm, D_MODEL), lambda i: (i, 0)),
        out_shape=jax.ShapeDtypeStruct((n, D_MODEL), f32),
        scratch_shapes=[pltpu.VMEM((tm, D_MODEL), bf16), pltpu.VMEM((tm, D_MODEL), f32)],
        compiler_params=_params(("parallel",)),
        name="ffn_ln",
    )(x, wgu, wd, g, b)


def _proj_kernel(dtypes, x_ref, *refs):
    xb = x_ref[...].astype(bf16)
    o_refs = iter(refs[len(dtypes):])
    for w_ref, dts in zip(refs[:len(dtypes)], dtypes):
        r = _dot(xb, w_ref[...])
        for dt in dts:
            next(o_refs)[...] = r.astype(dt)


def _proj(x, ws, dtypes, tm):
    n = x.shape[0]
    outs = [(w.shape[1], dt) for w, dts in zip(ws, dtypes) for dt in dts]
    return pl.pallas_call(
        functools.partial(_proj_kernel, dtypes),
        grid=(n // tm,),
        in_specs=[pl.BlockSpec((tm, D_MODEL), lambda i: (i, 0))] + [_whole(w.shape) for w in ws],
        out_specs=[pl.BlockSpec((tm, c), lambda i: (i, 0)) for c, _ in outs],
        out_shape=[jax.ShapeDtypeStruct((n, c), dt) for c, dt in outs],
        compiler_params=_params(("parallel",)),
        name="in_proj",
    )(x, *ws)


def _conv_kernel(tt, n_tiles, a_ref, b_ref, prev_ref, dw_ref, db_ref, g_ref, bb_ref,
                 c_ref, st_ref, ext_ref):
    j = pl.program_id(1)

    @pl.when(j == 0)
    def _():
        ext_ref[0:STATE_ROWS, :] = prev_ref[0]

    ext_ref[STATE_ROWS:STATE_ROWS + tt, :] = a_ref[0] * _sigmoid(b_ref[0])
    acc = jnp.zeros((tt, CONV_CH), f32)
    for tap in range(CONV_WIDTH):
        acc = acc + ext_ref[STATE_PAD + tap:STATE_PAD + tap + tt, :] * dw_ref[tap:tap + 1, :]
    y = _layer_norm(acc + db_ref[...], g_ref[...], bb_ref[...])
    c_ref[0] = y * _sigmoid(y)
    tail = ext_ref[tt:tt + STATE_ROWS, :]
    st_ref[0] = tail
    if n_tiles > 1:
        ext_ref[0:STATE_ROWS, :] = tail


def _conv_module(a, b, prev, dw, db, g, bb, tt):
    bsz, t, _ = a.shape
    n_tiles = t // tt
    blk = pl.BlockSpec((1, tt, CONV_CH), lambda i, j: (i, j, 0))
    st = pl.BlockSpec((1, STATE_ROWS, CONV_CH), lambda i, j: (i, 0, 0))
    return pl.pallas_call(
        functools.partial(_conv_kernel, tt, n_tiles),
        grid=(bsz, n_tiles),
        in_specs=[blk, blk, st, _whole(dw.shape), _whole(db.shape), _whole(g.shape), _whole(bb.shape)],
        out_specs=[blk, st],
        out_shape=[jax.ShapeDtypeStruct((bsz, t, CONV_CH), f32),
                   jax.ShapeDtypeStruct((bsz, STATE_ROWS, CONV_CH), f32)],
        scratch_shapes=[pltpu.VMEM((STATE_ROWS + tt, CONV_CH), f32)],
        compiler_params=_params(("parallel", "arbitrary")),
        name="conv_module",
    )(a, b, prev, dw, db, g, bb)


def _mem_attn_kernel(q_ref, mk_ref, mv_ref, o_ref):
    scale = MEM_HEAD_DIM ** -0.5
    for h in range(MEM_HEADS):
        sl = slice(h * MEM_HEAD_DIM, (h + 1) * MEM_HEAD_DIM)
        qh = (q_ref[0, :, sl] * scale).astype(bf16)
        kh = mk_ref[0, :, sl].astype(bf16)
        vh = mv_ref[0, :, sl].astype(bf16)
        lg = _dot_nt(qh, kh)
        mx = jnp.max(lg, axis=-1, keepdims=True)
        p = jnp.exp(lg - mx)
        den = jnp.sum(p, axis=-1, keepdims=True)
        o_ref[0, :, sl] = _dot((p / den).astype(bf16), vh)


def _mem_attn(q, mk, mv, tq):
    bsz, t, w = q.shape
    qblk = pl.BlockSpec((1, tq, w), lambda i, j: (i, j, 0))
    mblk = pl.BlockSpec((1, MEM_LEN, w), lambda i, j: (i, 0, 0))
    return pl.pallas_call(
        _mem_attn_kernel,
        grid=(bsz, t // tq),
        in_specs=[qblk, mblk, mblk],
        out_specs=qblk,
        out_shape=jax.ShapeDtypeStruct((bsz, t, w), f32),
        compiler_params=_params(("parallel", "parallel")),
        name="mem_attn",
    )(q, mk, mv)


def _merge_ln_kernel(x_ref, a_ref, c_ref, m_ref, g0_ref, g1_ref, g2_ref,
                     wa_ref, wc_ref, wm_ref, wo_ref, g_ref, b_ref, o_ref):
    y = _sigmoid(g0_ref[...]) * _dot(a_ref[...].astype(bf16), wa_ref[...])
    y = y + _sigmoid(g1_ref[...]) * _dot(c_ref[...].astype(bf16), wc_ref[...])
    y = y + _sigmoid(g2_ref[...]) * _dot(m_ref[...].astype(bf16), wm_ref[...])
    z = _dot(y.astype(bf16), wo_ref[...])
    o_ref[...] = _layer_norm(ALPHA * x_ref[...] + z, g_ref[...], b_ref[...])


def _merge_ln(x, a, c, m, g0, g1, g2, wa, wc, wm, wo, g, b, tm):
    n = x.shape[0]
    wide = pl.BlockSpec((tm, D_MODEL), lambda i: (i, 0))
    half = pl.BlockSpec((tm, ATTN_W), lambda i: (i, 0))
    return pl.pallas_call(
        _merge_ln_kernel,
        grid=(n // tm,),
        in_specs=[wide, half, half, half, wide, wide, wide,
                  _whole(wa.shape), _whole(wc.shape), _whole(wm.shape), _whole(wo.shape),
                  _whole(g.shape), _whole(b.shape)],
        out_specs=wide,
        out_shape=jax.ShapeDtypeStruct((n, D_MODEL), f32),
        compiler_params=_params(("parallel",)),
        name="merge_ln",
    )(x, a, c, m, g0, g1, g2, wa, wc, wm, wo, g, b)


def _rel_bucket_np(n):
    n = np.maximum(n, 0)
    exact = N_BUCKETS // 2
    nf = np.maximum(n, 1).astype(np.float64)
    large = exact + (np.log(nf / exact) / math.log(MAX_DISTANCE / exact) * (N_BUCKETS - exact)).astype(np.int64)
    large = np.minimum(large, N_BUCKETS - 1)
    return np.where(n < exact, n, large).astype(np.int32)


def _bias_kernel(rb_ref, bkt_ref, o_ref):
    n_tiles = bkt_ref.shape[0]
    for t in range(n_tiles):
        bkt = bkt_ref[t]
        for h in range(N_HEADS):
            far = rb_ref[N_BUCKETS - 1, h]
            acc = jnp.zeros(bkt.shape, f32)
            for bucket in range(N_BUCKETS - 1):
                acc = jnp.where(bkt == bucket, (rb_ref[bucket, h] - far) * LOG2E, acc)
            o_ref[t, h] = acc


def _bias_tiles(rel_bias, buckets):
    n_tiles, r, c = buckets.shape
    return pl.pallas_call(
        _bias_kernel,
        in_specs=[pl.BlockSpec(memory_space=pltpu.SMEM), _whole(buckets.shape)],
        out_specs=_whole((n_tiles, N_HEADS, r, c)),
        out_shape=jax.ShapeDtypeStruct((n_tiles, N_HEADS, r, c), f32),
        grid=(1,),
        name="rel_bias_tiles",
    )(rel_bias, buckets)


ACC_ROWS = HEAD_DIM + BF16_ROWS


def _dsa_prompt_kernel(q_ref, iq_ref, ikw_ref, k_ref, vt_ref, ik2_ref, bias_ref, o_ref,
                       s_ref, hi_ref, lo_ref, qp_ref, iqp_ref, lg_ref, m_ref, acc_ref):
    j = pl.program_id(1)
    n_chunks = j + 1
    kpos0 = lax.broadcasted_iota(jnp.int32, (TQ, TQ), 0)
    qpos0 = lax.broadcasted_iota(jnp.int32, (TQ, TQ), 1)

    def chunk_rows(c):
        return pl.ds(pl.multiple_of(c * TQ, TQ), TQ)

    lane = lax.broadcasted_iota(jnp.int32, (TQ, LANES), 1)
    for h in range(N_HEADS):
        pair = slice((h // 2) * LANES, (h // 2 + 1) * LANES)
        mine = (lane < HEAD_DIM) if h % 2 == 0 else (lane >= HEAD_DIM)
        qp_ref[h] = jnp.where(mine, q_ref[0, :, pair] * QK_SCALE, 0.0).astype(bf16)
        iqp_ref[h] = jnp.where(mine, iq_ref[0, :, pair], jnp.zeros((), bf16))

    w = ikw_ref[0].T[IDX_DIM:IDX_DIM + IDX_HEADS, :] * IDX_SCALE

    def score_chunk(c, carry):
        ikc = ik2_ref[0, chunk_rows(c), :]
        s = jnp.zeros((TQ, TQ), f32)
        for h in range(IDX_HEADS):
            d = _dot_nt(ikc, iqp_ref[h])
            s = s + jnp.maximum(d, 0.0) * w[h:h + 1, :]
        s = jnp.where(s == 0.0, 0.0, s)
        s = jnp.where(kpos0 + (c - j) * TQ <= qpos0, s, NEG_INF)
        s_ref[chunk_rows(c), :] = s
        bits = lax.bitcast_convert_type(s, jnp.int32)
        key = bits ^ (lax.shift_right_arithmetic(bits, 31) & 0x7FFFFFFF)
        hi_ref[chunk_rows(c), :] = lax.shift_right_arithmetic(key, 16).astype(jnp.int16)
        lo_ref[chunk_rows(c), :] = ((key & 0xFFFF) - HALF16).astype(jnp.int16)
        return carry

    lax.fori_loop(0, n_chunks, score_chunk, 0)

    def column_count(hit_fn):
        def body(c, cnt):
            hit = jnp.where(hit_fn(c, s_ref[chunk_rows(c), :]), 1, 0)
            return cnt + jnp.sum(hit.reshape(TQ // SUBLANES, SUBLANES, TQ), axis=0)

        cnt = lax.fori_loop(0, n_chunks, body, jnp.zeros((SUBLANES, TQ), jnp.int32))
        return jnp.sum(cnt, axis=0, keepdims=True)

    one16, zero16 = jnp.ones((), jnp.int16), jnp.zeros((), jnp.int16)

    never = jnp.full((TQ, TQ), -HALF16, jnp.int16)
    hi_ref[chunk_rows(n_chunks), :] = never
    lo_ref[chunk_rows(n_chunks), :] = never
    n_pairs = (n_chunks + 1) // 2

    def search16(ref, t_cnt):
        def count_ge(cand):
            def body(u, cnt):
                rows = pl.ds(pl.multiple_of(u * (2 * TQ), 2 * TQ), 2 * TQ)
                hit = jnp.where(ref[rows, :] >= cand, one16, zero16)
                for r in range(2 * TQ // BF16_ROWS):
                    cnt = cnt + hit[r * BF16_ROWS:(r + 1) * BF16_ROWS]
                return cnt

            cnt = lax.fori_loop(0, n_pairs, body, jnp.zeros((BF16_ROWS, TQ), jnp.int16))
            return jnp.sum(cnt.astype(jnp.int32), axis=0, keepdims=True)

        def step(i, carry):
            t, tc = carry
            cand = t + lax.shift_left(jnp.int32(1), 15 - i)
            cnt = count_ge(cand.astype(jnp.int16))
            ok = cnt >= TOPK
            return jnp.where(ok, cand, t), jnp.where(ok, cnt, tc)

        return lax.fori_loop(0, 16, step, (jnp.full((1, TQ), -HALF16, jnp.int32), t_cnt))

    t_hi, t_cnt = search16(hi_ref, jnp.full((1, TQ), 1 << 30, jnp.int32))
    t_hi16 = t_hi.astype(jnp.int16)

    def narrow(c, carry):
        hi = hi_ref[chunk_rows(c), :]
        lo_ref[chunk_rows(c), :] = jnp.where(
            hi == t_hi16, lo_ref[chunk_rows(c), :],
            jnp.where(hi > t_hi16, jnp.full((), HALF16 - 1, jnp.int16), jnp.full((), -HALF16, jnp.int16)))
        return carry

    lax.fori_loop(0, n_chunks, narrow, 0)
    t_lo, t_cnt = search16(lo_ref, t_cnt)
    t_key = t_hi * (2 * HALF16) + (t_lo + HALF16)
    has_k = t_key > KEY_NEG_INF
    thr = _key_to_float(jnp.maximum(t_key, KEY_NEG_INF + 1))

    tied = jnp.logical_and(has_k, t_cnt > TOPK)

    @pl.when(jnp.max(jnp.where(tied, 1, 0)) > 0)
    def _():
        def count_eq_below(pos):
            return column_count(
                lambda c, blk: jnp.logical_and(blk == thr, kpos0 + c * TQ < pos))

        n_eq = count_eq_below(jnp.full((1, TQ), 1 << 30, jnp.int32))
        need = jnp.where(tied, TOPK - (t_cnt - n_eq), 1 << 30)

        def psearch(i, pos):
            cand = pos + lax.shift_left(jnp.int32(1), 12 - i)
            return jnp.where(count_eq_below(cand) <= need - 1, cand, pos)

        pmax = lax.fori_loop(0, 13, psearch, jnp.zeros((1, TQ), jnp.int32))

        def drop(c, carry):
            s = s_ref[chunk_rows(c), :]
            s_ref[chunk_rows(c), :] = jnp.where(
                jnp.logical_and(s == thr, kpos0 + c * TQ > pmax), NEG_INF, s)
            return carry

        lax.fori_loop(0, n_chunks, drop, 0)

    def to_mask(c, carry):
        s_ref[chunk_rows(c), :] = jnp.where(s_ref[chunk_rows(c), :] >= thr, 0.0, NEG_INF)
        return carry

    lax.fori_loop(0, n_chunks, to_mask, 0)

    m_ref[...] = jnp.full(m_ref.shape, NEG_INF, f32)
    acc_ref[...] = jnp.zeros(acc_ref.shape, f32)
    ones = jnp.ones((BF16_ROWS, TQ), bf16)

    def attend_chunk(c, bias_tile):
        rows = chunk_rows(c)
        mask = s_ref[rows, :]
        stats = []
        for h in range(N_HEADS):
            pair = (h // 2) * LANES
            lg = _dot_nt(k_ref[0, rows, pair:pair + LANES], qp_ref[h]) + mask
            if bias_tile is not None:
                lg = lg + bias_ref[bias_tile, h]
            lg_ref[h] = lg
            m_old = m_ref[h]
            m_new = jnp.maximum(m_old, jnp.max(lg, axis=0, keepdims=True))
            m_safe = jnp.where(m_new == NEG_INF, 0.0, m_new)
            m_ref[h] = m_new
            stats.append((m_safe, jnp.exp2(m_old - m_safe)))
        for h in range(N_HEADS):
            m_safe, alpha = stats[h]
            p = jnp.exp2(lg_ref[h] - m_safe).astype(bf16)
            va = jnp.concatenate([vt_ref[0, h * HEAD_DIM:(h + 1) * HEAD_DIM, rows], ones], axis=0)
            acc_ref[h] = alpha * acc_ref[h] + _dot(va, p)

    def far_chunk(c, carry):
        attend_chunk(c, None)
        return carry

    lax.fori_loop(0, jnp.maximum(j - 1, 0), far_chunk, 0)

    @pl.when(j >= 1)
    def _():
        attend_chunk(j - 1, 1)

    attend_chunk(j, 0)

    for pair in range(N_HEADS // 2):
        even, odd = acc_ref[2 * pair], acc_ref[2 * pair + 1]
        out_t = jnp.concatenate([even[0:HEAD_DIM] / even[HEAD_DIM:HEAD_DIM + 1],
                                 odd[0:HEAD_DIM] / odd[HEAD_DIM:HEAD_DIM + 1]], axis=0)
        o_ref[0, :, pair * LANES:(pair + 1) * LANES] = out_t.T


def _dsa_prompt(q, iqb, ikw, kb, vt, ik2, bias):
    bsz, s, _ = q.shape
    once = pl.Buffered(1)
    qblk = lambda width: pl.BlockSpec((1, TQ, width), lambda b, j: (b, j, 0))
    return pl.pallas_call(
        _dsa_prompt_kernel,
        grid=(bsz, s // TQ),
        in_specs=[qblk(ATTN_W), qblk(IDX_HEADS * IDX_DIM), qblk(LANES),
                  pl.BlockSpec((1, s, ATTN_W), lambda b, j: (b, 0, 0), pipeline_mode=once),
                  pl.BlockSpec((1, ATTN_W, s), lambda b, j: (b, 0, 0), pipeline_mode=once),
                  pl.BlockSpec((1, s, LANES), lambda b, j: (b, 0, 0), pipeline_mode=once),
                  _whole(bias.shape)],
        out_specs=qblk(ATTN_W),
        out_shape=jax.ShapeDtypeStruct((bsz, s, ATTN_W), f32),
        scratch_shapes=[pltpu.VMEM((s, TQ), f32),
                        pltpu.VMEM((s + TQ, TQ), jnp.int16),
                        pltpu.VMEM((s + TQ, TQ), jnp.int16),
                        pltpu.VMEM((N_HEADS, TQ, LANES), bf16),
                        pltpu.VMEM((IDX_HEADS, TQ, LANES), bf16),
                        pltpu.VMEM((N_HEADS, TQ, TQ), f32),
                        pltpu.VMEM((N_HEADS, 1, TQ), f32),
                        pltpu.VMEM((N_HEADS, ACC_ROWS, TQ), f32)],
        compiler_params=_params(("parallel", "arbitrary")),
        name="dsa_prompt",
    )(q, iqb, ikw, kb, vt, ik2, bias)


N_PAGES = PAST_LEN // PAGE_SIZE
N_KV_GROUPS = N_PAGES // KV_GROUP_PAGES
S_KEYS = PAST_LEN + NEW_PAD


def _softmax_max(lg, m_ref, h):
    m_old = m_ref[h]
    m_new = jnp.maximum(m_old, jnp.max(lg, axis=-1, keepdims=True))
    m_safe = jnp.where(m_new == NEG_INF, 0.0, m_new)
    m_ref[h] = m_new
    return m_safe, jnp.exp2(m_old - m_safe)


def _softmax_acc(lg, m_safe, alpha, l_ref, acc_ref, h, vt):
    p = jnp.exp2(lg - m_safe[:, 0:1])
    l_ref[h] = alpha * l_ref[h] + jnp.sum(p, axis=-1, keepdims=True)
    acc_ref[h] = alpha[:, 0:HEAD_DIM] * acc_ref[h] + _dot_nt(p.astype(bf16), vt)


def _dsa_sample_kernel(layer, pt_ref, q_ref, iq_ref, ikw_ref, knt_ref, vnt_ref, iknt_ref,
                       bias_ref, cik_ref, ck_ref, cv_ref, o_ref,
                       ikbuf, kbuf, vbuf, s_ref, m_ref, l_ref, acc_ref, sem_ik, sem_kv):
    b = pl.program_id(0)
    tp = SAMPLE_T_PAD

    def page_lanes(i):
        return pl.ds(pl.multiple_of(i * PAGE_SIZE, PAGE_SIZE), PAGE_SIZE)

    def ik_copy(p, phys):
        return pltpu.make_async_copy(cik_ref.at[layer, phys], ikbuf.at[:, page_lanes(p)], sem_ik.at[0])

    def kv_copies(slot, i, phys):
        return (pltpu.make_async_copy(ck_ref.at[layer, phys], kbuf.at[slot, :, :, page_lanes(i)], sem_kv.at[slot]),
                pltpu.make_async_copy(cv_ref.at[layer, phys], vbuf.at[slot, :, :, page_lanes(i)], sem_kv.at[slot]))

    def start_group(g, slot):
        def body(i, carry):
            for cp in kv_copies(slot, i, pt_ref[b, g * KV_GROUP_PAGES + i]):
                cp.start()
            return carry
        lax.fori_loop(0, KV_GROUP_PAGES, body, 0)

    def wait_group(slot):
        def body(i, carry):
            for cp in kv_copies(slot, i, 0):
                cp.wait()
            return carry
        lax.fori_loop(0, KV_GROUP_PAGES, body, 0)

    def start_ik(p, carry):
        ik_copy(p, pt_ref[b, p]).start()
        return carry

    def wait_ik(p, carry):
        ik_copy(p, 0).wait()
        return carry

    lax.fori_loop(0, N_PAGES, start_ik, 0)
    start_group(0, 0)
    lax.fori_loop(0, N_PAGES, wait_ik, 0)

    ikw = ikw_ref[0]
    row = lax.broadcasted_iota(jnp.int32, (tp, NEW_PAD), 0)
    col = lax.broadcasted_iota(jnp.int32, (tp, NEW_PAD), 1)
    for g in range(N_KV_GROUPS):
        ikc = ikbuf[:, g * KV_GROUP:(g + 1) * KV_GROUP].astype(bf16)
        s = jnp.zeros((tp, KV_GROUP), f32)
        for h in range(IDX_HEADS):
            d = _dot(iq_ref[0, h], ikc)
            s = s + jnp.maximum(d, 0.0) * (ikw[:, IDX_DIM + h:IDX_DIM + h + 1] * IDX_SCALE)
        s_ref[:, g * KV_GROUP:(g + 1) * KV_GROUP] = s
    s = jnp.zeros((tp, NEW_PAD), f32)
    for h in range(IDX_HEADS):
        d = _dot(iq_ref[0, h], iknt_ref[0])
        s = s + jnp.maximum(d, 0.0) * (ikw[:, IDX_DIM + h:IDX_DIM + h + 1] * IDX_SCALE)
    s_ref[:, PAST_LEN:] = jnp.where(col <= row, s, NEG_INF)

    def count_ge(thr):
        hit = jnp.where(s_ref[...] >= thr, 1, 0)
        return jnp.sum(hit, axis=-1, keepdims=True)

    def search(i, carry):
        t_key, t_cnt = carry
        cand = t_key + lax.shift_left(jnp.int32(1), 31 - i)
        cnt = count_ge(_key_to_float(cand))
        ok = cnt >= TOPK
        return jnp.where(ok, cand, t_key), jnp.where(ok, cnt, t_cnt)

    t_key, t_cnt = lax.fori_loop(
        0, 32, search,
        (jnp.full((tp, 1), INT_MIN, jnp.int32), jnp.full((tp, 1), 1 << 30, jnp.int32)))
    has_k = t_key > KEY_NEG_INF
    thr = _key_to_float(jnp.maximum(t_key, KEY_NEG_INF + 1))
    tied = jnp.logical_and(has_k, t_cnt > TOPK)

    @pl.when(jnp.max(jnp.where(tied, 1, 0)) > 0)
    def _():
        kpos = lax.broadcasted_iota(jnp.int32, (tp, S_KEYS), 1)

        def count_eq_below(pos):
            hit = jnp.where(jnp.logical_and(s_ref[...] == thr, kpos < pos), 1, 0)
            return jnp.sum(hit, axis=-1, keepdims=True)

        n_eq = count_eq_below(jnp.full((tp, 1), 1 << 30, jnp.int32))
        need = jnp.where(tied, TOPK - (t_cnt - n_eq), 1 << 30)

        def psearch(i, pos):
            cand = pos + lax.shift_left(jnp.int32(1), 13 - i)
            return jnp.where(count_eq_below(cand) <= need - 1, cand, pos)

        pmax = lax.fori_loop(0, 14, psearch, jnp.zeros((tp, 1), jnp.int32))
        s = s_ref[...]
        s_ref[...] = jnp.where(jnp.logical_and(s == thr, kpos > pmax), NEG_INF, s)

    m_ref[...] = jnp.full(m_ref.shape, NEG_INF, f32)
    l_ref[...] = jnp.zeros(l_ref.shape, f32)
    acc_ref[...] = jnp.zeros(acc_ref.shape, f32)

    def group(g, carry):
        slot = g % 2

        @pl.when(g + 1 < N_KV_GROUPS)
        def _():
            start_group(g + 1, 1 - slot)

        wait_group(slot)
        ks = pl.multiple_of(g * KV_GROUP, KV_GROUP)
        sel = s_ref[:, pl.ds(ks, KV_GROUP)] >= thr
        near = jnp.where(g == N_KV_GROUPS - 1, 1.0, 0.0)
        logits, stats = [], []
        for h in range(N_HEADS):
            qh = (q_ref[0, h] * QK_SCALE).astype(bf16)
            lg = _dot(qh, kbuf[slot, h].astype(bf16)) + near * bias_ref[0, h]
            lg = jnp.where(sel, lg, NEG_INF)
            logits.append(lg)
            stats.append(_softmax_max(lg, m_ref, h))
        for h in range(N_HEADS):
            _softmax_acc(logits[h], *stats[h], l_ref, acc_ref, h, vbuf[slot, h].astype(bf16))
        return carry

    lax.fori_loop(0, N_KV_GROUPS, group, 0)

    sel = s_ref[:, PAST_LEN:] >= thr
    logits, stats = [], []
    for h in range(N_HEADS):
        qh = (q_ref[0, h] * QK_SCALE).astype(bf16)
        lg = _dot(qh, knt_ref[0, h]) + bias_ref[1, h, :, 0:NEW_PAD]
        lg = jnp.where(sel, lg, NEG_INF)
        logits.append(lg)
        stats.append(_softmax_max(lg, m_ref, h))
    for h in range(N_HEADS):
        _softmax_acc(logits[h], *stats[h], l_ref, acc_ref, h, vnt_ref[0, h])
        o_ref[0, :, h * HEAD_DIM:(h + 1) * HEAD_DIM] = acc_ref[h] / l_ref[h][:, 0:HEAD_DIM]


def _dsa_sample(layer, page_table, qh, iqh, ikw, knt, vnt, iknt, bias, cik_t, ck_t, cv_t):
    bsz = qh.shape[0]
    tp = SAMPLE_T_PAD
    any_spec = pl.BlockSpec(memory_space=pl.ANY)
    grid_spec = pltpu.PrefetchScalarGridSpec(
        num_scalar_prefetch=1,
        grid=(bsz,),
        in_specs=[pl.BlockSpec((1, N_HEADS, tp, HEAD_DIM), lambda b, pt: (b, 0, 0, 0)),
                  pl.BlockSpec((1, IDX_HEADS, tp, IDX_DIM), lambda b, pt: (b, 0, 0, 0)),
                  pl.BlockSpec((1, tp, LANES), lambda b, pt: (b, 0, 0)),
                  pl.BlockSpec((1, N_HEADS, HEAD_DIM, NEW_PAD), lambda b, pt: (b, 0, 0, 0)),
                  pl.BlockSpec((1, N_HEADS, HEAD_DIM, NEW_PAD), lambda b, pt: (b, 0, 0, 0)),
                  pl.BlockSpec((1, IDX_DIM, NEW_PAD), lambda b, pt: (b, 0, 0)),
                  pl.BlockSpec(bias.shape, lambda b, pt: (0, 0, 0, 0)),
                  any_spec, any_spec, any_spec],
        out_specs=pl.BlockSpec((1, tp, ATTN_W), lambda b, pt: (b, 0, 0)),
        scratch_shapes=[pltpu.VMEM((IDX_DIM, PAST_LEN), f32),
                        pltpu.VMEM((2, N_HEADS, HEAD_DIM, KV_GROUP), f32),
                        pltpu.VMEM((2, N_HEADS, HEAD_DIM, KV_GROUP), f32),
                        pltpu.VMEM((tp, S_KEYS), f32),
                        pltpu.VMEM((N_HEADS, tp, LANES), f32),
                        pltpu.VMEM((N_HEADS, tp, LANES), f32),
                        pltpu.VMEM((N_HEADS, tp, HEAD_DIM), f32),
                        pltpu.SemaphoreType.DMA((1,)),
                        pltpu.SemaphoreType.DMA((2,))],
    )
    return pl.pallas_call(
        functools.partial(_dsa_sample_kernel, layer),
        grid_spec=grid_spec,
        out_shape=jax.ShapeDtypeStruct((bsz, tp, ATTN_W), f32),
        compiler_params=_params(("arbitrary",)),
        name="dsa_sample",
    )(page_table, qh, iqh, ikw, knt, vnt, iknt, bias, cik_t, ck_t, cv_t)


def _prompt_buckets():
    k = np.arange(TQ)[:, None]
    q = np.arange(TQ)[None, :]
    return np.stack([_rel_bucket_np(q - k), _rel_bucket_np(TQ + q - k)])


def _sample_buckets():
    t = np.arange(SAMPLE_T_PAD)[:, None]
    k = np.arange(KV_GROUP)[None, :]
    last = _rel_bucket_np(PAST_LEN + t - ((N_KV_GROUPS - 1) * KV_GROUP + k))
    new = np.full((SAMPLE_T_PAD, KV_GROUP), N_BUCKETS - 1, np.int32)
    new[:, :NEW_PAD] = _rel_bucket_np(t - np.arange(NEW_PAD)[None, :])
    return np.stack([last, new])


def _heads_major(x, bsz, t, heads, dim):
    return x.reshape(bsz, t, heads, dim).transpose(0, 2, 1, 3)


def kernel(x_prompt, x_sample, mem_prompt, cache_k, cache_v, cache_idx_k, state_conv, cache_mem_k, cache_mem_v, page_table, ln_g, ln_b, w_ff_gate, w_ff_up, w_ff_down, w_in, conv_dw, conv_db, conv_ln_g, conv_ln_b, w_mem_kv, w_br_attn, w_br_conv, w_br_mem, w_o, rel_bias):
    bp, sp, _ = x_prompt.shape
    bs, ts, _ = x_sample.shape
    depth = w_in.shape[0]

    wg = w_ff_gate.astype(bf16).reshape(depth, 2, D_MODEL, N_FF_CHUNKS, FF_CHUNK).transpose(0, 1, 3, 2, 4)
    wu = w_ff_up.astype(bf16).reshape(depth, 2, D_MODEL, N_FF_CHUNKS, FF_CHUNK).transpose(0, 1, 3, 2, 4)
    wgu = jnp.concatenate([wg, wu], axis=-1)
    wd = w_ff_down.astype(bf16).reshape(depth, 2, N_FF_CHUNKS, FF_CHUNK, D_MODEL)
    w_inb = w_in.astype(bf16)
    edges = np.cumsum([0, ATTN_W, ATTN_W, ATTN_W, IDX_HEADS * IDX_DIM, IDX_DIM + IDX_HEADS,
                       CONV_CH, CONV_CH, MEM_HEADS * MEM_HEAD_DIM, D_MODEL, D_MODEL, D_MODEL])

    def in_weights(l):
        ws = [w_inb[l, :, edges[i]:edges[i + 1]] for i in range(len(edges) - 1)]
        w_ik = ws[4][:, :IDX_DIM]
        ws[4] = jnp.pad(ws[4], ((0, 0), (0, LANES - ws[4].shape[1])))
        ws.append(jnp.concatenate([w_ik, w_ik], axis=1))
        return ws

    in_dtypes = ((f32,), (f32, bf16), (f32,), (bf16,), (f32,), (f32,), (f32,), (f32,),
                 (f32,), (f32,), (f32,), (bf16,))

    w_memb = w_mem_kv.astype(bf16)
    wab, wcb, wmb, wob = (w.astype(bf16) for w in (w_br_attn, w_br_conv, w_br_mem, w_o))
    lng = ln_g.reshape(depth, 3, 1, D_MODEL)
    lnb = ln_b.reshape(depth, 3, 1, D_MODEL)

    bias_p = _bias_tiles(rel_bias, jnp.asarray(_prompt_buckets()))
    bias_s = _bias_tiles(rel_bias, jnp.asarray(_sample_buckets()))

    ck_t = cache_k.transpose(0, 1, 3, 4, 2)
    cv_t = cache_v.transpose(0, 1, 3, 4, 2)
    cik_t = cache_idx_k.transpose(0, 1, 3, 2)

    def layer(x, l, bsz, t, tm, mix):
        x = _ffn_ln(x, wgu[l, 0], wd[l, 0], lng[l, 0], lnb[l, 0], tm)
        q, k, kb, v, iqb, ikw, cua, cub, mq, g0, g1, g2, ik2 = _proj(
            x, in_weights(l), in_dtypes, min(tm, 256))
        a, c, cst, m, extra = mix(q, k, kb, v, iqb, ikw, ik2, cua, cub, mq)
        x = _merge_ln(x, a, c, m, g0, g1, g2, wab[l], wcb[l], wmb[l], wob[l],
                      lng[l, 1], lnb[l, 1], min(tm, 256))
        x = _ffn_ln(x, wgu[l, 1], wd[l, 1], lng[l, 2], lnb[l, 2], tm)
        state = (k.reshape(bsz, t, N_HEADS, HEAD_DIM), v.reshape(bsz, t, N_HEADS, HEAD_DIM),
                 ikw[:, :IDX_DIM].reshape(bsz, t, IDX_DIM), cst[:, STATE_PAD:]) + extra
        return x, state

    def conv_args(l):
        return (conv_dw[l], conv_db[l].reshape(1, CONV_CH), conv_ln_g[l].reshape(1, CONV_CH),
                conv_ln_b[l].reshape(1, CONV_CH))

    def prompt_mix(l):
        def mix(q, k, kb, v, iqb, ikw, ik2, cua, cub, mq):
            vt = v.reshape(bp, sp, ATTN_W).transpose(0, 2, 1).astype(bf16)
            a = _dsa_prompt(q.reshape(bp, sp, ATTN_W), iqb.reshape(bp, sp, IDX_HEADS * IDX_DIM),
                            ikw.reshape(bp, sp, LANES), kb.reshape(bp, sp, ATTN_W), vt,
                            ik2.reshape(bp, sp, LANES), bias_p)
            prev = jnp.zeros((bp, STATE_ROWS, CONV_CH), f32)
            c, cst = _conv_module(cua.reshape(bp, sp, CONV_CH), cub.reshape(bp, sp, CONV_CH), prev,
                                  *conv_args(l), 512)
            mem = mem_prompt.reshape(bp * MEM_LEN, D_MODEL)
            mk, mv = _proj(mem, [w_memb[l, :, :ATTN_W], w_memb[l, :, ATTN_W:]], ((f32,), (f32,)), 256)
            m = _mem_attn(mq.reshape(bp, sp, ATTN_W), mk.reshape(bp, MEM_LEN, ATTN_W),
                          mv.reshape(bp, MEM_LEN, ATTN_W), 512)
            extra = (mk.reshape(bp, MEM_LEN, MEM_HEADS, MEM_HEAD_DIM),
                     mv.reshape(bp, MEM_LEN, MEM_HEADS, MEM_HEAD_DIM))
            return (a.reshape(bp * sp, ATTN_W), c.reshape(bp * sp, CONV_CH), cst,
                    m.reshape(bp * sp, ATTN_W), extra)
        return mix

    def pad_to(x, axis, size):
        pad = [(0, 0)] * x.ndim
        pad[axis] = (0, size - x.shape[axis])
        return jnp.pad(x, pad)

    def sample_mix(l):
        def mix(q, k, kb, v, iqb, ikw, ik2, cua, cub, mq):
            tp = SAMPLE_T_PAD
            qh = pad_to(_heads_major(q, bs, ts, N_HEADS, HEAD_DIM), 2, tp)
            iqh = pad_to(_heads_major(iqb, bs, ts, IDX_HEADS, IDX_DIM), 2, tp)
            knt = pad_to(k.reshape(bs, ts, N_HEADS, HEAD_DIM).transpose(0, 2, 3, 1), 3, NEW_PAD).astype(bf16)
            vnt = pad_to(v.reshape(bs, ts, N_HEADS, HEAD_DIM).transpose(0, 2, 3, 1), 3, NEW_PAD).astype(bf16)
            iknt = pad_to(ikw[:, :IDX_DIM].reshape(bs, ts, IDX_DIM).transpose(0, 2, 1), 2, NEW_PAD).astype(bf16)
            ikw3 = pad_to(ikw.reshape(bs, ts, LANES), 1, tp)
            a = _dsa_sample(l, page_table, qh, iqh, ikw3, knt, vnt, iknt, bias_s,
                            cik_t, ck_t, cv_t)[:, :ts]
            prev = jnp.pad(state_conv[l], ((0, 0), (STATE_PAD, 0), (0, 0)))
            c, cst = _conv_module(cua.reshape(bs, ts, CONV_CH), cub.reshape(bs, ts, CONV_CH), prev,
                                  *conv_args(l), ts)
            mqp = pad_to(mq.reshape(bs, ts, ATTN_W), 1, tp)
            m = _mem_attn(mqp, cache_mem_k[l].reshape(bs, MEM_LEN, ATTN_W),
                          cache_mem_v[l].reshape(bs, MEM_LEN, ATTN_W), tp)[:, :ts]
            return (a.reshape(bs * ts, ATTN_W), c.reshape(bs * ts, CONV_CH), cst,
                    m.reshape(bs * ts, ATTN_W), ())
        return mix

    xp = x_prompt.reshape(bp * sp, D_MODEL)
    xs = x_sample.reshape(bs * ts, D_MODEL)
    st_p, st_s = [], []
    for l in range(depth):
        xp, s_p = layer(xp, l, bp, sp, 1024, prompt_mix(l))
        xs, s_s = layer(xs, l, bs, ts, bs * ts, sample_mix(l))
        st_p.append(s_p)
        st_s.append(s_s)

    outs_p = [jnp.stack([s[i] for s in st_p]) for i in range(6)]
    outs_s = [jnp.stack([s[i] for s in st_s]) for i in range(4)]
    return (xp.reshape(bp, sp, D_MODEL), xs.reshape(bs, ts, D_MODEL), *outs_p, *outs_s)
```

```python
import functools
import math

import numpy as np
import jax
import jax.numpy as jnp
from jax import lax
from jax.experimental import pallas as pl
from jax.experimental.pallas import tpu as pltpu

D_MODEL = 1024
N_HEADS = 8
HEAD_DIM = 64
ATTN_W = 512
IDX_HEADS = 8
IDX_DIM = 64
TOPK = 256
CONV_CH = 512
CONV_WIDTH = 31
MEM_LEN = 256
MEM_HEADS = 4
MEM_HEAD_DIM = 128
D_FF = 2816
N_BUCKETS = 32
MAX_DISTANCE = 128
LN_EPS = 1e-5
DEPTH = 2
ALPHA = (2 * DEPTH) ** 0.25
PAGE_SIZE = 128
PAST_LEN = 8192

LANES = 128
SUBLANES = 8
BF16_ROWS = 16
VMEM_LIMIT = 56 * 1024 * 1024

FF_CHUNK = 256
N_FF_CHUNKS = D_FF // FF_CHUNK
TQ = 256
FAR_CHUNKS = 2
STATE_ROWS = 32
STATE_PAD = STATE_ROWS - (CONV_WIDTH - 1)
SAMPLE_T_PAD = 8
NEW_PAD = 128
KV_GROUP_PAGES = 8
KV_GROUP = KV_GROUP_PAGES * PAGE_SIZE

IDX_SCALE = (IDX_DIM ** -0.5) * (IDX_HEADS ** -0.5)
LOG2E = math.log2(math.e)
QK_SCALE = (HEAD_DIM ** -0.5) * LOG2E
INT_MIN = -(2 ** 31)
HALF16 = 2 ** 15
KEY_NEG_INF = -2139095041
NEG_INF = float("-inf")

bf16 = jnp.bfloat16
f32 = jnp.float32


def _dot(a, b):
    return jnp.dot(a, b, preferred_element_type=f32)


def _dot_nt(a, b):
    return lax.dot_general(a, b, (((1,), (1,)), ((), ())), preferred_element_type=f32)


def _sigmoid(x):
    return 1.0 / (1.0 + jnp.exp(-x))


def _layer_norm(y, g, b):
    mu = jnp.mean(y, axis=-1, keepdims=True)
    d = y - mu
    var = jnp.mean(d * d, axis=-1, keepdims=True)
    return d * lax.rsqrt(var + LN_EPS) * g + b


def _key_to_float(key):
    bits = key ^ (lax.shift_right_arithmetic(key, 31) & 0x7FFFFFFF)
    return lax.bitcast_convert_type(bits, f32)


def _params(sem):
    return pltpu.CompilerParams(dimension_semantics=sem, vmem_limit_bytes=VMEM_LIMIT)


def _whole(shape):
    nd = len(shape)
    return pl.BlockSpec(shape, lambda *_: (0,) * nd)


LN_ROWS = 128


def _ffn_ln_kernel(x_ref, wgu_ref, wd_ref, g_ref, b_ref, o_ref, xb_ref, acc_ref):
    xb_ref[...] = x_ref[...].astype(bf16)
    acc_ref[...] = jnp.zeros_like(acc_ref)

    def chunk(c, carry):
        gu = _dot(xb_ref[...], wgu_ref[c])
        gate, up = gu[:, :FF_CHUNK], gu[:, FF_CHUNK:]
        h = (gate * _sigmoid(gate) * up).astype(bf16)
        acc_ref[...] += _dot(h, wd_ref[c])
        return carry

    lax.fori_loop(0, N_FF_CHUNKS, chunk, 0)

    def post_norm(r, carry):
        rows = pl.ds(pl.multiple_of(r * LN_ROWS, LN_ROWS), LN_ROWS)
        y = ALPHA * x_ref[rows, :] + 0.5 * acc_ref[rows, :]
        o_ref[rows, :] = _layer_norm(y, g_ref[...], b_ref[...])
        return carry

    lax.fori_loop(0, x_ref.shape[0] // LN_ROWS, post_norm, 0)


def _ffn_ln(x, wgu, wd, g, b, tm):
    n = x.shape[0]
    return pl.pallas_call(
        _ffn_ln_kernel,
        grid=(n // tm,),
        in_specs=[pl.BlockSpec((tm, D_MODEL), lambda i: (i, 0)),
                  _whole(wgu.shape), _whole(wd.shape), _whole(g.shape), _whole(b.shape)],
        out_specs=pl.BlockSpec((tm, D_MODEL), lambda i: (i, 0)),
        out_shape=jax.ShapeDtypeStruct((n, D_MODEL), f32),
        scratch_shapes=[pltpu.VMEM((tm, D_MODEL), bf16), pltpu.VMEM((tm, D_MODEL), f32)],
        compiler_params=_params(("parallel",)),
        name="ffn_ln",
    )(x, wgu, wd, g, b)


def _proj_kernel(plan, x_ref, *refs):
    xb = x_ref[...].astype(bf16)
    o_refs = iter(refs[len(plan):])
    for w_ref, (feature_major, dts) in zip(refs[:len(plan)], plan):
        r = _dot_nt(w_ref[...], xb) if feature_major else _dot(xb, w_ref[...])
        for dt in dts:
            o_ref = next(o_refs)
            o_ref[...] = r.astype(dt).reshape(o_ref.shape)


def _proj(x, ws, plan, tm, seq):
    n = x.shape[0]
    per_seq = seq // tm
    specs, shapes = [], []
    for w, (feature_major, dts) in zip(ws, plan):
        for dt in dts:
            if feature_major:
                specs.append(pl.BlockSpec((1, w.shape[0], tm), lambda i: (i // per_seq, 0, i % per_seq)))
                shapes.append(jax.ShapeDtypeStruct((n // seq, w.shape[0], seq), dt))
            else:
                specs.append(pl.BlockSpec((tm, w.shape[1]), lambda i: (i, 0)))
                shapes.append(jax.ShapeDtypeStruct((n, w.shape[1]), dt))
    return pl.pallas_call(
        functools.partial(_proj_kernel, plan),
        grid=(n // tm,),
        in_specs=[pl.BlockSpec((tm, D_MODEL), lambda i: (i, 0))] + [_whole(w.shape) for w in ws],
        out_specs=specs,
        out_shape=shapes,
        compiler_params=_params(("parallel",)),
        name="in_proj",
    )(x, *ws)


def _conv_kernel(tt, n_tiles, a_ref, b_ref, prev_ref, dw_ref, db_ref, g_ref, bb_ref,
                 c_ref, st_ref, ext_ref):
    j = pl.program_id(1)

    @pl.when(j == 0)
    def _():
        ext_ref[0:STATE_ROWS, :] = prev_ref[0]

    ext_ref[STATE_ROWS:STATE_ROWS + tt, :] = a_ref[0] * _sigmoid(b_ref[0])
    acc = jnp.zeros((tt, CONV_CH), f32)
    for tap in range(CONV_WIDTH):
        acc = acc + ext_ref[STATE_PAD + tap:STATE_PAD + tap + tt, :] * dw_ref[tap:tap + 1, :]
    y = _layer_norm(acc + db_ref[...], g_ref[...], bb_ref[...])
    c_ref[0] = y * _sigmoid(y)
    tail = ext_ref[tt:tt + STATE_ROWS, :]
    st_ref[0] = tail
    if n_tiles > 1:
        ext_ref[0:STATE_ROWS, :] = tail


def _conv_module(a, b, prev, dw, db, g, bb, tt):
    bsz, t, _ = a.shape
    n_tiles = t // tt
    blk = pl.BlockSpec((1, tt, CONV_CH), lambda i, j: (i, j, 0))
    st = pl.BlockSpec((1, STATE_ROWS, CONV_CH), lambda i, j: (i, 0, 0))
    return pl.pallas_call(
        functools.partial(_conv_kernel, tt, n_tiles),
        grid=(bsz, n_tiles),
        in_specs=[blk, blk, st, _whole(dw.shape), _whole(db.shape), _whole(g.shape), _whole(bb.shape)],
        out_specs=[blk, st],
        out_shape=[jax.ShapeDtypeStruct((bsz, t, CONV_CH), f32),
                   jax.ShapeDtypeStruct((bsz, STATE_ROWS, CONV_CH), f32)],
        scratch_shapes=[pltpu.VMEM((STATE_ROWS + tt, CONV_CH), f32)],
        compiler_params=_params(("parallel", "arbitrary")),
        name="conv_module",
    )(a, b, prev, dw, db, g, bb)


def _mem_attn_kernel(q_ref, mk_ref, mv_ref, o_ref):
    scale = MEM_HEAD_DIM ** -0.5
    for h in range(MEM_HEADS):
        sl = slice(h * MEM_HEAD_DIM, (h + 1) * MEM_HEAD_DIM)
        qh = (q_ref[0, :, sl] * scale).astype(bf16)
        kh = mk_ref[0, :, sl].astype(bf16)
        vh = mv_ref[0, :, sl].astype(bf16)
        lg = _dot_nt(qh, kh)
        mx = jnp.max(lg, axis=-1, keepdims=True)
        p = jnp.exp(lg - mx)
        den = jnp.sum(p, axis=-1, keepdims=True)
        o_ref[0, :, sl] = _dot((p / den).astype(bf16), vh)


def _mem_attn(q, mk, mv, tq):
    bsz, t, w = q.shape
    qblk = pl.BlockSpec((1, tq, w), lambda i, j: (i, j, 0))
    mblk = pl.BlockSpec((1, MEM_LEN, w), lambda i, j: (i, 0, 0))
    return pl.pallas_call(
        _mem_attn_kernel,
        grid=(bsz, t // tq),
        in_specs=[qblk, mblk, mblk],
        out_specs=qblk,
        out_shape=jax.ShapeDtypeStruct((bsz, t, w), f32),
        compiler_params=_params(("parallel", "parallel")),
        name="mem_attn",
    )(q, mk, mv)


def _merge_ln_kernel(x_ref, a_ref, c_ref, m_ref, g0_ref, g1_ref, g2_ref,
                     wa_ref, wc_ref, wm_ref, wo_ref, g_ref, b_ref, o_ref):
    y = _sigmoid(g0_ref[...]) * _dot(a_ref[...].astype(bf16), wa_ref[...])
    y = y + _sigmoid(g1_ref[...]) * _dot(c_ref[...].astype(bf16), wc_ref[...])
    y = y + _sigmoid(g2_ref[...]) * _dot(m_ref[...].astype(bf16), wm_ref[...])
    z = _dot(y.astype(bf16), wo_ref[...])
    o_ref[...] = _layer_norm(ALPHA * x_ref[...] + z, g_ref[...], b_ref[...])


def _merge_ln(x, a, c, m, g0, g1, g2, wa, wc, wm, wo, g, b, tm):
    n = x.shape[0]
    wide = pl.BlockSpec((tm, D_MODEL), lambda i: (i, 0))
    half = pl.BlockSpec((tm, ATTN_W), lambda i: (i, 0))
    return pl.pallas_call(
        _merge_ln_kernel,
        grid=(n // tm,),
        in_specs=[wide, half, half, half, wide, wide, wide,
                  _whole(wa.shape), _whole(wc.shape), _whole(wm.shape), _whole(wo.shape),
                  _whole(g.shape), _whole(b.shape)],
        out_specs=wide,
        out_shape=jax.ShapeDtypeStruct((n, D_MODEL), f32),
        compiler_params=_params(("parallel",)),
        name="merge_ln",
    )(x, a, c, m, g0, g1, g2, wa, wc, wm, wo, g, b)


def _rel_bucket_np(n):
    n = np.maximum(n, 0)
    exact = N_BUCKETS // 2
    nf = np.maximum(n, 1).astype(np.float64)
    large = exact + (np.log(nf / exact) / math.log(MAX_DISTANCE / exact) * (N_BUCKETS - exact)).astype(np.int64)
    large = np.minimum(large, N_BUCKETS - 1)
    return np.where(n < exact, n, large).astype(np.int32)


def _bias_kernel(rb_ref, bkt_ref, o_ref):
    n_tiles = bkt_ref.shape[0]
    for t in range(n_tiles):
        bkt = bkt_ref[t]
        for h in range(N_HEADS):
            far = rb_ref[N_BUCKETS - 1, h]
            acc = jnp.zeros(bkt.shape, f32)
            for bucket in range(N_BUCKETS - 1):
                acc = jnp.where(bkt == bucket, (rb_ref[bucket, h] - far) * LOG2E, acc)
            o_ref[t, h] = acc


def _bias_tiles(rel_bias, buckets):
    n_tiles, r, c = buckets.shape
    return pl.pallas_call(
        _bias_kernel,
        in_specs=[pl.BlockSpec(memory_space=pltpu.SMEM), _whole(buckets.shape)],
        out_specs=_whole((n_tiles, N_HEADS, r, c)),
        out_shape=jax.ShapeDtypeStruct((n_tiles, N_HEADS, r, c), f32),
        grid=(1,),
        name="rel_bias_tiles",
    )(rel_bias, buckets)


ACC_ROWS = HEAD_DIM + BF16_ROWS


def _dsa_prompt_kernel(q_ref, iq_ref, ikw_ref, k_ref, vt_ref, ik2_ref, bias_ref, o_ref,
                       s_ref, hi_ref, lo_ref, qp_ref, iqp_ref, lg_ref, m_ref, acc_ref):
    j = pl.program_id(1)
    n_chunks = j + 1
    kpos0 = lax.broadcasted_iota(jnp.int32, (TQ, TQ), 0)
    qpos0 = lax.broadcasted_iota(jnp.int32, (TQ, TQ), 1)

    def chunk_rows(c):
        return pl.ds(pl.multiple_of(c * TQ, TQ), TQ)

    lane = lax.broadcasted_iota(jnp.int32, (TQ, LANES), 1)
    for h in range(N_HEADS):
        pair = slice((h // 2) * LANES, (h // 2 + 1) * LANES)
        mine = (lane < HEAD_DIM) if h % 2 == 0 else (lane >= HEAD_DIM)
        qp_ref[h] = jnp.where(mine, q_ref[0, :, pair] * QK_SCALE, 0.0).astype(bf16)
        iqp_ref[h] = jnp.where(mine, iq_ref[0, :, pair], jnp.zeros((), bf16))

    w = ikw_ref[0].T[IDX_DIM:IDX_DIM + IDX_HEADS, :] * IDX_SCALE

    def score_chunk(c, carry):
        ikc = ik2_ref[0, chunk_rows(c), :]
        s = jnp.zeros((TQ, TQ), f32)
        for h in range(IDX_HEADS):
            d = _dot_nt(ikc, iqp_ref[h])
            s = s + jnp.maximum(d, 0.0) * w[h:h + 1, :]
        s = jnp.where(s == 0.0, 0.0, s)
        s = jnp.where(kpos0 + (c - j) * TQ <= qpos0, s, NEG_INF)
        s_ref[chunk_rows(c), :] = s
        bits = lax.bitcast_convert_type(s, jnp.int32)
        key = bits ^ (lax.shift_right_arithmetic(bits, 31) & 0x7FFFFFFF)
        hi_ref[chunk_rows(c), :] = lax.shift_right_arithmetic(key, 16).astype(jnp.int16)
        lo_ref[chunk_rows(c), :] = ((key & 0xFFFF) - HALF16).astype(jnp.int16)
        return carry

    lax.fori_loop(0, n_chunks, score_chunk, 0)

    def column_count(hit_fn):
        def body(c, cnt):
            hit = jnp.where(hit_fn(c, s_ref[chunk_rows(c), :]), 1, 0)
            return cnt + jnp.sum(hit.reshape(TQ // SUBLANES, SUBLANES, TQ), axis=0)

        cnt = lax.fori_loop(0, n_chunks, body, jnp.zeros((SUBLANES, TQ), jnp.int32))
        return jnp.sum(cnt, axis=0, keepdims=True)

    one16, zero16 = jnp.ones((), jnp.int16), jnp.zeros((), jnp.int16)

    never = jnp.full((TQ, TQ), -HALF16, jnp.int16)
    hi_ref[chunk_rows(n_chunks), :] = never
    lo_ref[chunk_rows(n_chunks), :] = never
    n_pairs = (n_chunks + 1) // 2

    def search16(ref, t_cnt):
        def count_ge(cand):
            def body(u, cnt):
                rows = pl.ds(pl.multiple_of(u * (2 * TQ), 2 * TQ), 2 * TQ)
                hit = jnp.where(ref[rows, :] >= cand, one16, zero16)
                for r in range(2 * TQ // BF16_ROWS):
                    cnt = cnt + hit[r * BF16_ROWS:(r + 1) * BF16_ROWS]
                return cnt

            cnt = lax.fori_loop(0, n_pairs, body, jnp.zeros((BF16_ROWS, TQ), jnp.int16))
            return jnp.sum(cnt.astype(jnp.int32), axis=0, keepdims=True)

        def step(i, carry):
            t, tc = carry
            cand = t + lax.shift_left(jnp.int32(1), 15 - i)
            cnt = count_ge(cand.astype(jnp.int16))
            ok = cnt >= TOPK
            return jnp.where(ok, cand, t), jnp.where(ok, cnt, tc)

        return lax.fori_loop(0, 16, step, (jnp.full((1, TQ), -HALF16, jnp.int32), t_cnt))

    t_hi, t_cnt = search16(hi_ref, jnp.full((1, TQ), 1 << 30, jnp.int32))
    t_hi16 = t_hi.astype(jnp.int16)

    def narrow(c, carry):
        hi = hi_ref[chunk_rows(c), :]
        lo_ref[chunk_rows(c), :] = jnp.where(
            hi == t_hi16, lo_ref[chunk_rows(c), :],
            jnp.where(hi > t_hi16, jnp.full((), HALF16 - 1, jnp.int16), jnp.full((), -HALF16, jnp.int16)))
        return carry

    lax.fori_loop(0, n_chunks, narrow, 0)
    t_lo, t_cnt = search16(lo_ref, t_cnt)
    t_key = t_hi * (2 * HALF16) + (t_lo + HALF16)
    has_k = t_key > KEY_NEG_INF
    thr = _key_to_float(jnp.maximum(t_key, KEY_NEG_INF + 1))

    tied = jnp.logical_and(has_k, t_cnt > TOPK)

    @pl.when(jnp.max(jnp.where(tied, 1, 0)) > 0)
    def _():
        def count_eq_below(pos):
            return column_count(
                lambda c, blk: jnp.logical_and(blk == thr, kpos0 + c * TQ < pos))

        n_eq = count_eq_below(jnp.full((1, TQ), 1 << 30, jnp.int32))
        need = jnp.where(tied, TOPK - (t_cnt - n_eq), 1 << 30)

        def psearch(i, pos):
            cand = pos + lax.shift_left(jnp.int32(1), 12 - i)
            return jnp.where(count_eq_below(cand) <= need - 1, cand, pos)

        pmax = lax.fori_loop(0, 13, psearch, jnp.zeros((1, TQ), jnp.int32))

        def drop(c, carry):
            s = s_ref[chunk_rows(c), :]
            s_ref[chunk_rows(c), :] = jnp.where(
                jnp.logical_and(s == thr, kpos0 + c * TQ > pmax), NEG_INF, s)
            return carry

        lax.fori_loop(0, n_chunks, drop, 0)

    def to_mask(c, carry):
        s_ref[chunk_rows(c), :] = jnp.where(s_ref[chunk_rows(c), :] >= thr, 0.0, NEG_INF)
        return carry

    lax.fori_loop(0, n_chunks, to_mask, 0)

    m_ref[...] = jnp.full(m_ref.shape, NEG_INF, f32)
    acc_ref[...] = jnp.zeros(acc_ref.shape, f32)
    def attend(c, n, bias_tile):
        rows = pl.ds(pl.multiple_of(c * TQ, TQ), n * TQ)
        mask = s_ref[rows, :]
        stats = []
        for h in range(N_HEADS):
            pair = (h // 2) * LANES
            lg = _dot_nt(k_ref[0, rows, pair:pair + LANES], qp_ref[h]) + mask
            if bias_tile is not None:
                lg = lg + bias_ref[bias_tile, h]
            lg_ref[h, 0:n * TQ, :] = lg
            m_old = m_ref[h]
            m_new = jnp.maximum(m_old, jnp.max(lg, axis=0, keepdims=True))
            m_safe = jnp.where(m_new == NEG_INF, 0.0, m_new)
            m_ref[h] = m_new
            stats.append((m_safe, jnp.exp2(m_old - m_safe)))
        ones = jnp.ones((BF16_ROWS, n * TQ), bf16)
        for h in range(N_HEADS):
            m_safe, alpha = stats[h]
            p = jnp.exp2(lg_ref[h, 0:n * TQ, :] - m_safe).astype(bf16)
            va = jnp.concatenate([vt_ref[0, h * HEAD_DIM:(h + 1) * HEAD_DIM, rows], ones], axis=0)
            acc_ref[h] = alpha * acc_ref[h] + _dot(va, p)

    n_far = jnp.maximum(j - 1, 0)

    def far_pair(u, carry):
        attend(2 * u, FAR_CHUNKS, None)
        return carry

    lax.fori_loop(0, n_far // FAR_CHUNKS, far_pair, 0)

    @pl.when(n_far % FAR_CHUNKS == 1)
    def _():
        attend(n_far - 1, 1, None)

    @pl.when(j >= 1)
    def _():
        attend(j - 1, 1, 1)

    attend(j, 1, 0)

    for pair in range(N_HEADS // 2):
        even, odd = acc_ref[2 * pair], acc_ref[2 * pair + 1]
        out_t = jnp.concatenate([even[0:HEAD_DIM] / even[HEAD_DIM:HEAD_DIM + 1],
                                 odd[0:HEAD_DIM] / odd[HEAD_DIM:HEAD_DIM + 1]], axis=0)
        o_ref[0, :, pair * LANES:(pair + 1) * LANES] = out_t.T


def _dsa_prompt(q, iqb, ikw, kb, vt, ik2, bias):
    bsz, s, _ = q.shape
    once = pl.Buffered(1)
    qblk = lambda width: pl.BlockSpec((1, TQ, width), lambda b, j: (b, j, 0))
    return pl.pallas_call(
        _dsa_prompt_kernel,
        grid=(bsz, s // TQ),
        in_specs=[qblk(ATTN_W), qblk(IDX_HEADS * IDX_DIM), qblk(LANES),
                  pl.BlockSpec((1, s, ATTN_W), lambda b, j: (b, 0, 0), pipeline_mode=once),
                  pl.BlockSpec((1, ATTN_W, s), lambda b, j: (b, 0, 0), pipeline_mode=once),
                  pl.BlockSpec((1, s, LANES), lambda b, j: (b, 0, 0), pipeline_mode=once),
                  _whole(bias.shape)],
        out_specs=qblk(ATTN_W),
        out_shape=jax.ShapeDtypeStruct((bsz, s, ATTN_W), f32),
        scratch_shapes=[pltpu.VMEM((s, TQ), f32),
                        pltpu.VMEM((s + TQ, TQ), jnp.int16),
                        pltpu.VMEM((s + TQ, TQ), jnp.int16),
                        pltpu.VMEM((N_HEADS, TQ, LANES), bf16),
                        pltpu.VMEM((IDX_HEADS, TQ, LANES), bf16),
                        pltpu.VMEM((N_HEADS, FAR_CHUNKS * TQ, TQ), f32),
                        pltpu.VMEM((N_HEADS, 1, TQ), f32),
                        pltpu.VMEM((N_HEADS, ACC_ROWS, TQ), f32)],
        compiler_params=_params(("parallel", "arbitrary")),
        name="dsa_prompt",
    )(q, iqb, ikw, kb, vt, ik2, bias)


N_PAGES = PAST_LEN // PAGE_SIZE
N_KV_GROUPS = N_PAGES // KV_GROUP_PAGES
S_KEYS = PAST_LEN + NEW_PAD


def _softmax_max(lg, m_ref, h):
    m_old = m_ref[h]
    m_new = jnp.maximum(m_old, jnp.max(lg, axis=-1, keepdims=True))
    m_safe = jnp.where(m_new == NEG_INF, 0.0, m_new)
    m_ref[h] = m_new
    return m_safe, jnp.exp2(m_old - m_safe)


def _softmax_acc(lg, m_safe, alpha, l_ref, acc_ref, h, vt):
    p = jnp.exp2(lg - m_safe[:, 0:1])
    l_ref[h] = alpha * l_ref[h] + jnp.sum(p, axis=-1, keepdims=True)
    acc_ref[h] = alpha[:, 0:HEAD_DIM] * acc_ref[h] + _dot_nt(p.astype(bf16), vt)


def _dsa_sample_kernel(layer, pt_ref, q_ref, iq_ref, ikw_ref, knt_ref, vnt_ref, iknt_ref,
                       bias_ref, cik_ref, ck_ref, cv_ref, o_ref,
                       ikbuf, kbuf, vbuf, s_ref, m_ref, l_ref, acc_ref, sem_ik, sem_kv):
    b = pl.program_id(0)
    tp = SAMPLE_T_PAD

    def page_lanes(i):
        return pl.ds(pl.multiple_of(i * PAGE_SIZE, PAGE_SIZE), PAGE_SIZE)

    def ik_copy(p, phys):
        return pltpu.make_async_copy(cik_ref.at[layer, phys], ikbuf.at[:, page_lanes(p)], sem_ik.at[0])

    def kv_copies(slot, i, phys):
        return (pltpu.make_async_copy(ck_ref.at[layer, phys], kbuf.at[slot, :, :, page_lanes(i)], sem_kv.at[slot]),
                pltpu.make_async_copy(cv_ref.at[layer, phys], vbuf.at[slot, :, :, page_lanes(i)], sem_kv.at[slot]))

    def start_group(g, carry):
        def body(i, c):
            for cp in kv_copies(g, i, pt_ref[b, g * KV_GROUP_PAGES + i]):
                cp.start()
            return c
        return lax.fori_loop(0, KV_GROUP_PAGES, body, carry)

    def wait_group(slot):
        def body(i, carry):
            for cp in kv_copies(slot, i, 0):
                cp.wait()
            return carry
        lax.fori_loop(0, KV_GROUP_PAGES, body, 0)

    def start_ik(p, carry):
        ik_copy(p, pt_ref[b, p]).start()
        return carry

    def wait_ik(p, carry):
        ik_copy(p, 0).wait()
        return carry

    lax.fori_loop(0, N_PAGES, start_ik, 0)
    lax.fori_loop(0, N_KV_GROUPS, start_group, 0)
    lax.fori_loop(0, N_PAGES, wait_ik, 0)

    ikw = ikw_ref[0]
    row = lax.broadcasted_iota(jnp.int32, (tp, NEW_PAD), 0)
    col = lax.broadcasted_iota(jnp.int32, (tp, NEW_PAD), 1)
    for g in range(N_KV_GROUPS):
        ikc = ikbuf[:, g * KV_GROUP:(g + 1) * KV_GROUP].astype(bf16)
        s = jnp.zeros((tp, KV_GROUP), f32)
        for h in range(IDX_HEADS):
            d = _dot(iq_ref[0, h], ikc)
            s = s + jnp.maximum(d, 0.0) * (ikw[:, IDX_DIM + h:IDX_DIM + h + 1] * IDX_SCALE)
        s_ref[:, g * KV_GROUP:(g + 1) * KV_GROUP] = s
    s = jnp.zeros((tp, NEW_PAD), f32)
    for h in range(IDX_HEADS):
        d = _dot(iq_ref[0, h], iknt_ref[0])
        s = s + jnp.maximum(d, 0.0) * (ikw[:, IDX_DIM + h:IDX_DIM + h + 1] * IDX_SCALE)
    s_ref[:, PAST_LEN:] = jnp.where(col <= row, s, NEG_INF)

    def count_ge(thr):
        hit = jnp.where(s_ref[...] >= thr, 1, 0)
        return jnp.sum(hit, axis=-1, keepdims=True)

    def search(i, carry):
        t_key, t_cnt = carry
        cand = t_key + lax.shift_left(jnp.int32(1), 31 - i)
        cnt = count_ge(_key_to_float(cand))
        ok = cnt >= TOPK
        return jnp.where(ok, cand, t_key), jnp.where(ok, cnt, t_cnt)

    t_key, t_cnt = lax.fori_loop(
        0, 32, search,
        (jnp.full((tp, 1), INT_MIN, jnp.int32), jnp.full((tp, 1), 1 << 30, jnp.int32)))
    has_k = t_key > KEY_NEG_INF
    thr = _key_to_float(jnp.maximum(t_key, KEY_NEG_INF + 1))
    tied = jnp.logical_and(has_k, t_cnt > TOPK)

    @pl.when(jnp.max(jnp.where(tied, 1, 0)) > 0)
    def _():
        kpos = lax.broadcasted_iota(jnp.int32, (tp, S_KEYS), 1)

        def count_eq_below(pos):
            hit = jnp.where(jnp.logical_and(s_ref[...] == thr, kpos < pos), 1, 0)
            return jnp.sum(hit, axis=-1, keepdims=True)

        n_eq = count_eq_below(jnp.full((tp, 1), 1 << 30, jnp.int32))
        need = jnp.where(tied, TOPK - (t_cnt - n_eq), 1 << 30)

        def psearch(i, pos):
            cand = pos + lax.shift_left(jnp.int32(1), 13 - i)
            return jnp.where(count_eq_below(cand) <= need - 1, cand, pos)

        pmax = lax.fori_loop(0, 14, psearch, jnp.zeros((tp, 1), jnp.int32))
        s = s_ref[...]
        s_ref[...] = jnp.where(jnp.logical_and(s == thr, kpos > pmax), NEG_INF, s)

    m_ref[...] = jnp.full(m_ref.shape, NEG_INF, f32)
    l_ref[...] = jnp.zeros(l_ref.shape, f32)
    acc_ref[...] = jnp.zeros(acc_ref.shape, f32)

    def group(g, carry):
        slot = g
        wait_group(slot)
        ks = pl.multiple_of(g * KV_GROUP, KV_GROUP)
        sel = s_ref[:, pl.ds(ks, KV_GROUP)] >= thr
        near = jnp.where(g == N_KV_GROUPS - 1, 1.0, 0.0)
        logits, stats = [], []
        for h in range(N_HEADS):
            qh = (q_ref[0, h] * QK_SCALE).astype(bf16)
            lg = _dot(qh, kbuf[slot, h].astype(bf16)) + near * bias_ref[0, h]
            lg = jnp.where(sel, lg, NEG_INF)
            logits.append(lg)
            stats.append(_softmax_max(lg, m_ref, h))
        for h in range(N_HEADS):
            _softmax_acc(logits[h], *stats[h], l_ref, acc_ref, h, vbuf[slot, h].astype(bf16))
        return carry

    lax.fori_loop(0, N_KV_GROUPS, group, 0)

    sel = s_ref[:, PAST_LEN:] >= thr
    logits, stats = [], []
    for h in range(N_HEADS):
        qh = (q_ref[0, h] * QK_SCALE).astype(bf16)
        lg = _dot(qh, knt_ref[0, h]) + bias_ref[1, h, :, 0:NEW_PAD]
        lg = jnp.where(sel, lg, NEG_INF)
        logits.append(lg)
        stats.append(_softmax_max(lg, m_ref, h))
    for h in range(N_HEADS):
        _softmax_acc(logits[h], *stats[h], l_ref, acc_ref, h, vnt_ref[0, h])
        o_ref[0, :, h * HEAD_DIM:(h + 1) * HEAD_DIM] = acc_ref[h] / l_ref[h][:, 0:HEAD_DIM]


def _dsa_sample(layer, page_table, qh, iqh, ikw, knt, vnt, iknt, bias, cik_t, ck_t, cv_t):
    bsz = qh.shape[0]
    tp = SAMPLE_T_PAD
    any_spec = pl.BlockSpec(memory_space=pl.ANY)
    grid_spec = pltpu.PrefetchScalarGridSpec(
        num_scalar_prefetch=1,
        grid=(bsz,),
        in_specs=[pl.BlockSpec((1, N_HEADS, tp, HEAD_DIM), lambda b, pt: (b, 0, 0, 0)),
                  pl.BlockSpec((1, IDX_HEADS, tp, IDX_DIM), lambda b, pt: (b, 0, 0, 0)),
                  pl.BlockSpec((1, tp, LANES), lambda b, pt: (b, 0, 0)),
                  pl.BlockSpec((1, N_HEADS, HEAD_DIM, NEW_PAD), lambda b, pt: (b, 0, 0, 0)),
                  pl.BlockSpec((1, N_HEADS, HEAD_DIM, NEW_PAD), lambda b, pt: (b, 0, 0, 0)),
                  pl.BlockSpec((1, IDX_DIM, NEW_PAD), lambda b, pt: (b, 0, 0)),
                  pl.BlockSpec(bias.shape, lambda b, pt: (0, 0, 0, 0)),
                  any_spec, any_spec, any_spec],
        out_specs=pl.BlockSpec((1, tp, ATTN_W), lambda b, pt: (b, 0, 0)),
        scratch_shapes=[pltpu.VMEM((IDX_DIM, PAST_LEN), f32),
                        pltpu.VMEM((N_KV_GROUPS, N_HEADS, HEAD_DIM, KV_GROUP), f32),
                        pltpu.VMEM((N_KV_GROUPS, N_HEADS, HEAD_DIM, KV_GROUP), f32),
                        pltpu.VMEM((tp, S_KEYS), f32),
                        pltpu.VMEM((N_HEADS, tp, LANES), f32),
                        pltpu.VMEM((N_HEADS, tp, LANES), f32),
                        pltpu.VMEM((N_HEADS, tp, HEAD_DIM), f32),
                        pltpu.SemaphoreType.DMA((1,)),
                        pltpu.SemaphoreType.DMA((N_KV_GROUPS,))],
    )
    return pl.pallas_call(
        functools.partial(_dsa_sample_kernel, layer),
        grid_spec=grid_spec,
        out_shape=jax.ShapeDtypeStruct((bsz, tp, ATTN_W), f32),
        compiler_params=_params(("arbitrary",)),
        name="dsa_sample",
    )(page_table, qh, iqh, ikw, knt, vnt, iknt, bias, cik_t, ck_t, cv_t)


def _prompt_buckets():
    k = np.arange(TQ)[:, None]
    q = np.arange(TQ)[None, :]
    return np.stack([_rel_bucket_np(q - k), _rel_bucket_np(TQ + q - k)])


def _sample_buckets():
    t = np.arange(SAMPLE_T_PAD)[:, None]
    k = np.arange(KV_GROUP)[None, :]
    last = _rel_bucket_np(PAST_LEN + t - ((N_KV_GROUPS - 1) * KV_GROUP + k))
    new = np.full((SAMPLE_T_PAD, KV_GROUP), N_BUCKETS - 1, np.int32)
    new[:, :NEW_PAD] = _rel_bucket_np(t - np.arange(NEW_PAD)[None, :])
    return np.stack([last, new])


def _heads_major(x, bsz, t, heads, dim):
    return x.reshape(bsz, t, heads, dim).transpose(0, 2, 1, 3)


def kernel(x_prompt, x_sample, mem_prompt, cache_k, cache_v, cache_idx_k, state_conv, cache_mem_k, cache_mem_v, page_table, ln_g, ln_b, w_ff_gate, w_ff_up, w_ff_down, w_in, conv_dw, conv_db, conv_ln_g, conv_ln_b, w_mem_kv, w_br_attn, w_br_conv, w_br_mem, w_o, rel_bias):
    bp, sp, _ = x_prompt.shape
    bs, ts, _ = x_sample.shape
    depth = w_in.shape[0]

    wg = w_ff_gate.astype(bf16).reshape(depth, 2, D_MODEL, N_FF_CHUNKS, FF_CHUNK).transpose(0, 1, 3, 2, 4)
    wu = w_ff_up.astype(bf16).reshape(depth, 2, D_MODEL, N_FF_CHUNKS, FF_CHUNK).transpose(0, 1, 3, 2, 4)
    wgu = jnp.concatenate([wg, wu], axis=-1)
    wd = w_ff_down.astype(bf16).reshape(depth, 2, N_FF_CHUNKS, FF_CHUNK, D_MODEL)
    w_inb = w_in.astype(bf16)
    edges = np.cumsum([0, ATTN_W, ATTN_W, ATTN_W, IDX_HEADS * IDX_DIM, IDX_DIM + IDX_HEADS,
                       CONV_CH, CONV_CH, MEM_HEADS * MEM_HEAD_DIM, D_MODEL, D_MODEL, D_MODEL])

    nat = lambda *dts: (False, dts)
    fm = lambda *dts: (True, dts)

    def project(x, l, seq, tm, prompt):
        wq, wk, wv, wiq, wikw, wca, wcb_, wmq, wg0, wg1, wg2 = (
            w_inb[l, :, edges[i]:edges[i + 1]] for i in range(len(edges) - 1))
        w_ik = wikw[:, :IDX_DIM]
        wikw = jnp.pad(wikw, ((0, 0), (0, LANES - wikw.shape[1])))
        names = ["q", "iqb", "ikw", "cua", "cub", "mq", "g0", "g1", "g2"]
        ws = [wq, wiq, wikw, wca, wcb_, wmq, wg0, wg1, wg2]
        plan = [nat(f32), nat(bf16), nat(f32)] + [nat(f32)] * 6
        if prompt:
            names += ["kb", "kt", "vt", "vtb", "ik2"]
            ws += [wk, wk.T, wv.T, jnp.concatenate([w_ik, w_ik], axis=1)]
            plan += [nat(bf16), fm(f32), fm(f32, bf16), nat(bf16)]
        else:
            names += ["k", "v"]
            ws += [wk, wv]
            plan += [nat(f32), nat(f32)]
        return dict(zip(names, _proj(x, ws, tuple(plan), tm, seq)))

    w_memb = w_mem_kv.astype(bf16)
    wab, wcb, wmb, wob = (w.astype(bf16) for w in (w_br_attn, w_br_conv, w_br_mem, w_o))
    lng = ln_g.reshape(depth, 3, 1, D_MODEL)
    lnb = ln_b.reshape(depth, 3, 1, D_MODEL)

    bias_p = _bias_tiles(rel_bias, jnp.asarray(_prompt_buckets()))
    bias_s = _bias_tiles(rel_bias, jnp.asarray(_sample_buckets()))

    ck_t = cache_k.transpose(0, 1, 3, 4, 2)
    cv_t = cache_v.transpose(0, 1, 3, 4, 2)
    cik_t = cache_idx_k.transpose(0, 1, 3, 2)

    def layer(x, l, bsz, t, tm, prompt, mix):
        x = _ffn_ln(x, wgu[l, 0], wd[l, 0], lng[l, 0], lnb[l, 0], tm)
        p = project(x, l, t if prompt else x.shape[0], min(tm, 256), prompt)
        a, c, m, state = mix(p)
        x = _merge_ln(x, a, c, m, p["g0"], p["g1"], p["g2"], wab[l], wcb[l], wmb[l], wob[l],
                      lng[l, 1], lnb[l, 1], min(tm, 256))
        x = _ffn_ln(x, wgu[l, 1], wd[l, 1], lng[l, 2], lnb[l, 2], tm)
        return x, state

    def conv_args(l):
        return (conv_dw[l], conv_db[l].reshape(1, CONV_CH), conv_ln_g[l].reshape(1, CONV_CH),
                conv_ln_b[l].reshape(1, CONV_CH))

    def prompt_mix(l):
        def mix(p):
            ikw = p["ikw"]
            a = _dsa_prompt(p["q"].reshape(bp, sp, ATTN_W), p["iqb"].reshape(bp, sp, IDX_HEADS * IDX_DIM),
                            ikw.reshape(bp, sp, LANES), p["kb"].reshape(bp, sp, ATTN_W), p["vtb"],
                            p["ik2"].reshape(bp, sp, LANES), bias_p)
            prev = jnp.zeros((bp, STATE_ROWS, CONV_CH), f32)
            c, cst = _conv_module(p["cua"].reshape(bp, sp, CONV_CH), p["cub"].reshape(bp, sp, CONV_CH),
                                  prev, *conv_args(l), 512)
            mem = mem_prompt.reshape(bp * MEM_LEN, D_MODEL)
            mk, mv = _proj(mem, [w_memb[l, :, :ATTN_W], w_memb[l, :, ATTN_W:]], (nat(f32), nat(f32)),
                           256, bp * MEM_LEN)
            m = _mem_attn(p["mq"].reshape(bp, sp, ATTN_W), mk.reshape(bp, MEM_LEN, ATTN_W),
                          mv.reshape(bp, MEM_LEN, ATTN_W), 512)
            state = (p["kt"], p["vt"], ikw[:, :IDX_DIM].reshape(bp, sp, IDX_DIM), cst[:, STATE_PAD:],
                     mk.reshape(bp, MEM_LEN, MEM_HEADS, MEM_HEAD_DIM),
                     mv.reshape(bp, MEM_LEN, MEM_HEADS, MEM_HEAD_DIM))
            return (a.reshape(bp * sp, ATTN_W), c.reshape(bp * sp, CONV_CH),
                    m.reshape(bp * sp, ATTN_W), state)
        return mix

    def pad_to(x, axis, size):
        pad = [(0, 0)] * x.ndim
        pad[axis] = (0, size - x.shape[axis])
        return jnp.pad(x, pad)

    def sample_mix(l):
        def mix(p):
            q, k, v, iqb, ikw, cua, cub, mq = (p[n] for n in ("q", "k", "v", "iqb", "ikw", "cua", "cub", "mq"))
            tp = SAMPLE_T_PAD
            qh = pad_to(_heads_major(q, bs, ts, N_HEADS, HEAD_DIM), 2, tp)
            iqh = pad_to(_heads_major(iqb, bs, ts, IDX_HEADS, IDX_DIM), 2, tp)
            knt = pad_to(k.reshape(bs, ts, N_HEADS, HEAD_DIM).transpose(0, 2, 3, 1), 3, NEW_PAD).astype(bf16)
            vnt = pad_to(v.reshape(bs, ts, N_HEADS, HEAD_DIM).transpose(0, 2, 3, 1), 3, NEW_PAD).astype(bf16)
            iknt = pad_to(ikw[:, :IDX_DIM].reshape(bs, ts, IDX_DIM).transpose(0, 2, 1), 2, NEW_PAD).astype(bf16)
            ikw3 = pad_to(ikw.reshape(bs, ts, LANES), 1, tp)
            a = _dsa_sample(l, page_table, qh, iqh, ikw3, knt, vnt, iknt, bias_s,
                            cik_t, ck_t, cv_t)[:, :ts]
            prev = jnp.pad(state_conv[l], ((0, 0), (STATE_PAD, 0), (0, 0)))
            c, cst = _conv_module(cua.reshape(bs, ts, CONV_CH), cub.reshape(bs, ts, CONV_CH), prev,
                                  *conv_args(l), ts)
            mqp = pad_to(mq.reshape(bs, ts, ATTN_W), 1, tp)
            m = _mem_attn(mqp, cache_mem_k[l].reshape(bs, MEM_LEN, ATTN_W),
                          cache_mem_v[l].reshape(bs, MEM_LEN, ATTN_W), tp)[:, :ts]
            state = (k.reshape(bs, ts, N_HEADS, HEAD_DIM), v.reshape(bs, ts, N_HEADS, HEAD_DIM),
                     ikw[:, :IDX_DIM].reshape(bs, ts, IDX_DIM), cst[:, STATE_PAD:])
            return (a.reshape(bs * ts, ATTN_W), c.reshape(bs * ts, CONV_CH),
                    m.reshape(bs * ts, ATTN_W), state)
        return mix

    xp = x_prompt.reshape(bp * sp, D_MODEL)
    xs = x_sample.reshape(bs * ts, D_MODEL)
    st_p, st_s = [], []
    for l in range(depth):
        xp, s_p = layer(xp, l, bp, sp, 1024, True, prompt_mix(l))
        xs, s_s = layer(xs, l, bs, ts, bs * ts, False, sample_mix(l))
        st_p.append(s_p)
        st_s.append(s_s)

    outs_p = [jnp.stack([s[i] for s in st_p]) for i in range(6)]
    for i in (0, 1):
        outs_p[i] = outs_p[i].reshape(depth, bp, N_HEADS, HEAD_DIM, sp).transpose(0, 1, 4, 2, 3)
    outs_s = [jnp.stack([s[i] for s in st_s]) for i in range(4)]
    return (xp.reshape(bp, sp, D_MODEL), xs.reshape(bs, ts, D_MODEL), *outs_p, *outs_s)
```

```python
import functools
import math

import numpy as np
import jax
import jax.numpy as jnp
from jax import lax
from jax.experimental import pallas as pl
from jax.experimental.pallas import tpu as pltpu

D_MODEL = 1024
N_HEADS = 8
HEAD_DIM = 64
ATTN_W = 512
IDX_HEADS = 8
IDX_DIM = 64
TOPK = 256
CONV_CH = 512
CONV_WIDTH = 31
MEM_LEN = 256
MEM_HEADS = 4
MEM_HEAD_DIM = 128
D_FF = 2816
N_BUCKETS = 32
MAX_DISTANCE = 128
LN_EPS = 1e-5
DEPTH = 2
ALPHA = (2 * DEPTH) ** 0.25
PAGE_SIZE = 128
PAST_LEN = 8192

LANES = 128
SUBLANES = 8
BF16_ROWS = 16
VMEM_LIMIT = 56 * 1024 * 1024

FF_CHUNK = 256
N_FF_CHUNKS = D_FF // FF_CHUNK
TQ = 256
FAR_CHUNKS = 2
COUNT_CHUNKS = 2
STATE_ROWS = 32
STATE_PAD = STATE_ROWS - (CONV_WIDTH - 1)
SAMPLE_T_PAD = 8
NEW_PAD = 128
KV_GROUP_PAGES = 8
KV_GROUP = KV_GROUP_PAGES * PAGE_SIZE

IDX_SCALE = (IDX_DIM ** -0.5) * (IDX_HEADS ** -0.5)
LOG2E = math.log2(math.e)
QK_SCALE = (HEAD_DIM ** -0.5) * LOG2E
INT_MIN = -(2 ** 31)
HALF16 = 2 ** 15
KEY_NEG_INF = -2139095041
NEG_INF = float("-inf")

bf16 = jnp.bfloat16
f32 = jnp.float32


def _dot(a, b):
    return jnp.dot(a, b, preferred_element_type=f32)


def _dot_nt(a, b):
    return lax.dot_general(a, b, (((1,), (1,)), ((), ())), preferred_element_type=f32)


def _sigmoid(x):
    return 1.0 / (1.0 + jnp.exp(-x))


def _layer_norm(y, g, b):
    mu = jnp.mean(y, axis=-1, keepdims=True)
    d = y - mu
    var = jnp.mean(d * d, axis=-1, keepdims=True)
    return d * lax.rsqrt(var + LN_EPS) * g + b


def _key_to_float(key):
    bits = key ^ (lax.shift_right_arithmetic(key, 31) & 0x7FFFFFFF)
    return lax.bitcast_convert_type(bits, f32)


def _params(sem):
    return pltpu.CompilerParams(dimension_semantics=sem, vmem_limit_bytes=VMEM_LIMIT)


def _whole(shape):
    nd = len(shape)
    return pl.BlockSpec(shape, lambda *_: (0,) * nd)


LN_ROWS = 128


def _ffn_ln_kernel(x_ref, wgu_ref, wd_ref, g_ref, b_ref, o_ref, xb_ref, acc_ref):
    xb_ref[...] = x_ref[...].astype(bf16)
    acc_ref[...] = jnp.zeros_like(acc_ref)

    def chunk(c, carry):
        gu = _dot(xb_ref[...], wgu_ref[c])
        gate, up = gu[:, :FF_CHUNK], gu[:, FF_CHUNK:]
        h = (gate * _sigmoid(gate) * up).astype(bf16)
        acc_ref[...] += _dot(h, wd_ref[c])
        return carry

    lax.fori_loop(0, N_FF_CHUNKS, chunk, 0)

    def post_norm(r, carry):
        rows = pl.ds(pl.multiple_of(r * LN_ROWS, LN_ROWS), LN_ROWS)
        y = ALPHA * x_ref[rows, :] + 0.5 * acc_ref[rows, :]
        o_ref[rows, :] = _layer_norm(y, g_ref[...], b_ref[...])
        return carry

    lax.fori_loop(0, x_ref.shape[0] // LN_ROWS, post_norm, 0)


def _ffn_ln(x, wgu, wd, g, b, tm):
    n = x.shape[0]
    return pl.pallas_call(
        _ffn_ln_kernel,
        grid=(n // tm,),
        in_specs=[pl.BlockSpec((tm, D_MODEL), lambda i: (i, 0)),
                  _whole(wgu.shape), _whole(wd.shape), _whole(g.shape), _whole(b.shape)],
        out_specs=pl.BlockSpec((tm, D_MODEL), lambda i: (i, 0)),
        out_shape=jax.ShapeDtypeStruct((n, D_MODEL), f32),
        scratch_shapes=[pltpu.VMEM((tm, D_MODEL), bf16), pltpu.VMEM((tm, D_MODEL), f32)],
        compiler_params=_params(("parallel",)),
        name="ffn_ln",
    )(x, wgu, wd, g, b)


def _proj_kernel(plan, x_ref, *refs):
    xb = x_ref[...].astype(bf16)
    o_refs = iter(refs[len(plan):])
    for w_ref, (feature_major, dts) in zip(refs[:len(plan)], plan):
        r = _dot_nt(w_ref[...], xb) if feature_major else _dot(xb, w_ref[...])
        for dt in dts:
            o_ref = next(o_refs)
            o_ref[...] = r.astype(dt).reshape(o_ref.shape)


def _proj(x, ws, plan, tm, seq):
    n = x.shape[0]
    per_seq = seq // tm
    specs, shapes = [], []
    for w, (feature_major, dts) in zip(ws, plan):
        for dt in dts:
            if feature_major:
                specs.append(pl.BlockSpec((1, w.shape[0], tm), lambda i: (i // per_seq, 0, i % per_seq)))
                shapes.append(jax.ShapeDtypeStruct((n // seq, w.shape[0], seq), dt))
            else:
                specs.append(pl.BlockSpec((tm, w.shape[1]), lambda i: (i, 0)))
                shapes.append(jax.ShapeDtypeStruct((n, w.shape[1]), dt))
    return pl.pallas_call(
        functools.partial(_proj_kernel, plan),
        grid=(n // tm,),
        in_specs=[pl.BlockSpec((tm, D_MODEL), lambda i: (i, 0))] + [_whole(w.shape) for w in ws],
        out_specs=specs,
        out_shape=shapes,
        compiler_params=_params(("parallel",)),
        name="in_proj",
    )(x, *ws)


CONV_ROWS = 64
SHIFT_TAIL = STATE_ROWS - SUBLANES


def _conv_kernel(tt, n_tiles, a_ref, b_ref, prev_ref, dw_ref, db_ref, g_ref, bb_ref,
                 c_ref, st_ref, ext_ref, sh_ref):
    j = pl.program_id(1)
    rb = min(tt, CONV_ROWS)

    @pl.when(j == 0)
    def _():
        ext_ref[0:STATE_ROWS, :] = prev_ref[0]

    ext_ref[STATE_ROWS:STATE_ROWS + tt, :] = a_ref[0] * _sigmoid(b_ref[0])
    for r in range(1, SUBLANES):
        sh_ref[r - 1] = ext_ref[r:r + tt + SHIFT_TAIL, :]

    def block(i, carry):
        base = i * rb
        acc = jnp.zeros((rb, CONV_CH), f32)
        for tap in range(CONV_WIDTH):
            off = STATE_PAD + tap
            rows = pl.ds(pl.multiple_of(base + off // SUBLANES * SUBLANES, min(rb, SUBLANES)), rb)
            src = ext_ref[rows, :] if off % SUBLANES == 0 else sh_ref[off % SUBLANES - 1, rows, :]
            acc = acc + src * dw_ref[tap:tap + 1, :]
        c_ref[0, pl.ds(pl.multiple_of(base, rb), rb), :] = acc
        return carry

    lax.fori_loop(0, tt // rb, block, 0)
    y = _layer_norm(c_ref[0] + db_ref[...], g_ref[...], bb_ref[...])
    c_ref[0] = y * _sigmoid(y)
    tail = ext_ref[tt:tt + STATE_ROWS, :]
    st_ref[0] = tail
    if n_tiles > 1:
        ext_ref[0:STATE_ROWS, :] = tail


def _conv_module(a, b, prev, dw, db, g, bb, tt):
    bsz, t, _ = a.shape
    n_tiles = t // tt
    blk = pl.BlockSpec((1, tt, CONV_CH), lambda i, j: (i, j, 0))
    st = pl.BlockSpec((1, STATE_ROWS, CONV_CH), lambda i, j: (i, 0, 0))
    return pl.pallas_call(
        functools.partial(_conv_kernel, tt, n_tiles),
        grid=(bsz, n_tiles),
        in_specs=[blk, blk, st, _whole(dw.shape), _whole(db.shape), _whole(g.shape), _whole(bb.shape)],
        out_specs=[blk, st],
        out_shape=[jax.ShapeDtypeStruct((bsz, t, CONV_CH), f32),
                   jax.ShapeDtypeStruct((bsz, STATE_ROWS, CONV_CH), f32)],
        scratch_shapes=[pltpu.VMEM((STATE_ROWS + tt, CONV_CH), f32),
                        pltpu.VMEM((SUBLANES - 1, tt + SHIFT_TAIL, CONV_CH), f32)],
        compiler_params=_params(("parallel", "arbitrary")),
        name="conv_module",
    )(a, b, prev, dw, db, g, bb)


def _mem_attn_kernel(q_ref, mk_ref, mv_ref, o_ref):
    scale = MEM_HEAD_DIM ** -0.5
    for h in range(MEM_HEADS):
        sl = slice(h * MEM_HEAD_DIM, (h + 1) * MEM_HEAD_DIM)
        qh = (q_ref[0, :, sl] * scale).astype(bf16)
        kh = mk_ref[0, :, sl].astype(bf16)
        vh = mv_ref[0, :, sl].astype(bf16)
        lg = _dot_nt(qh, kh)
        mx = jnp.max(lg, axis=-1, keepdims=True)
        p = jnp.exp(lg - mx)
        den = jnp.sum(p, axis=-1, keepdims=True)
        o_ref[0, :, sl] = _dot((p / den).astype(bf16), vh)


def _mem_attn(q, mk, mv, tq):
    bsz, t, w = q.shape
    qblk = pl.BlockSpec((1, tq, w), lambda i, j: (i, j, 0))
    mblk = pl.BlockSpec((1, MEM_LEN, w), lambda i, j: (i, 0, 0))
    return pl.pallas_call(
        _mem_attn_kernel,
        grid=(bsz, t // tq),
        in_specs=[qblk, mblk, mblk],
        out_specs=qblk,
        out_shape=jax.ShapeDtypeStruct((bsz, t, w), f32),
        compiler_params=_params(("parallel", "parallel")),
        name="mem_attn",
    )(q, mk, mv)


def _merge_ln_kernel(x_ref, a_ref, c_ref, m_ref, wg_ref, wa_ref, wc_ref, wm_ref, wo_ref,
                     g_ref, b_ref, o_ref):
    xb = x_ref[...].astype(bf16)
    y = _sigmoid(_dot(xb, wg_ref[0])) * _dot(a_ref[...].astype(bf16), wa_ref[...])
    y = y + _sigmoid(_dot(xb, wg_ref[1])) * _dot(c_ref[...].astype(bf16), wc_ref[...])
    y = y + _sigmoid(_dot(xb, wg_ref[2])) * _dot(m_ref[...].astype(bf16), wm_ref[...])
    z = _dot(y.astype(bf16), wo_ref[...])
    o_ref[...] = _layer_norm(ALPHA * x_ref[...] + z, g_ref[...], b_ref[...])


def _merge_ln(x, a, c, m, wg, wa, wc, wm, wo, g, b, tm):
    n = x.shape[0]
    wide = pl.BlockSpec((tm, D_MODEL), lambda i: (i, 0))
    half = pl.BlockSpec((tm, ATTN_W), lambda i: (i, 0))
    return pl.pallas_call(
        _merge_ln_kernel,
        grid=(n // tm,),
        in_specs=[wide, half, half, half, _whole(wg.shape),
                  _whole(wa.shape), _whole(wc.shape), _whole(wm.shape), _whole(wo.shape),
                  _whole(g.shape), _whole(b.shape)],
        out_specs=wide,
        out_shape=jax.ShapeDtypeStruct((n, D_MODEL), f32),
        compiler_params=_params(("parallel",)),
        name="merge_ln",
    )(x, a, c, m, wg, wa, wc, wm, wo, g, b)


def _rel_bucket_np(n):
    n = np.maximum(n, 0)
    exact = N_BUCKETS // 2
    nf = np.maximum(n, 1).astype(np.float64)
    large = exact + (np.log(nf / exact) / math.log(MAX_DISTANCE / exact) * (N_BUCKETS - exact)).astype(np.int64)
    large = np.minimum(large, N_BUCKETS - 1)
    return np.where(n < exact, n, large).astype(np.int32)


def _bias_kernel(rb_ref, bkt_ref, o_ref):
    n_tiles = bkt_ref.shape[0]
    for t in range(n_tiles):
        bkt = bkt_ref[t]
        for h in range(N_HEADS):
            far = rb_ref[N_BUCKETS - 1, h]
            acc = jnp.zeros(bkt.shape, f32)
            for bucket in range(N_BUCKETS - 1):
                acc = jnp.where(bkt == bucket, (rb_ref[bucket, h] - far) * LOG2E, acc)
            o_ref[t, h] = acc


def _bias_tiles(rel_bias, buckets):
    n_tiles, r, c = buckets.shape
    return pl.pallas_call(
        _bias_kernel,
        in_specs=[pl.BlockSpec(memory_space=pltpu.SMEM), _whole(buckets.shape)],
        out_specs=_whole((n_tiles, N_HEADS, r, c)),
        out_shape=jax.ShapeDtypeStruct((n_tiles, N_HEADS, r, c), f32),
        grid=(1,),
        name="rel_bias_tiles",
    )(rel_bias, buckets)


ACC_ROWS = HEAD_DIM + BF16_ROWS


def _dsa_prompt_kernel(q_ref, iq_ref, ikw_ref, k_ref, vt_ref, ik2_ref, bias_ref, o_ref,
                       s_ref, hi_ref, lo_ref, qp_ref, iqp_ref, lg_ref, m_ref, acc_ref):
    j = pl.program_id(1)
    n_chunks = j + 1
    kpos0 = lax.broadcasted_iota(jnp.int32, (TQ, TQ), 0)
    qpos0 = lax.broadcasted_iota(jnp.int32, (TQ, TQ), 1)

    def chunk_rows(c):
        return pl.ds(pl.multiple_of(c * TQ, TQ), TQ)

    lane = lax.broadcasted_iota(jnp.int32, (TQ, LANES), 1)
    for h in range(N_HEADS):
        pair = slice((h // 2) * LANES, (h // 2 + 1) * LANES)
        mine = (lane < HEAD_DIM) if h % 2 == 0 else (lane >= HEAD_DIM)
        qp_ref[h] = jnp.where(mine, q_ref[0, :, pair] * QK_SCALE, 0.0).astype(bf16)
        iqp_ref[h] = jnp.where(mine, iq_ref[0, :, pair], jnp.zeros((), bf16))

    w = ikw_ref[0].T[IDX_DIM:IDX_DIM + IDX_HEADS, :] * IDX_SCALE

    def score_chunk(c, carry):
        ikc = ik2_ref[0, chunk_rows(c), :]
        s = jnp.zeros((TQ, TQ), f32)
        for h in range(IDX_HEADS):
            d = _dot_nt(ikc, iqp_ref[h])
            s = s + jnp.maximum(d, 0.0) * w[h:h + 1, :]
        s = jnp.where(s == 0.0, 0.0, s)
        s = jnp.where(kpos0 + (c - j) * TQ <= qpos0, s, NEG_INF)
        s_ref[chunk_rows(c), :] = s
        bits = lax.bitcast_convert_type(s, jnp.int32)
        key = bits ^ (lax.shift_right_arithmetic(bits, 31) & 0x7FFFFFFF)
        hi_ref[chunk_rows(c), :] = lax.shift_right_arithmetic(key, 16).astype(jnp.int16)
        lo_ref[chunk_rows(c), :] = ((key & 0xFFFF) - HALF16).astype(jnp.int16)
        return carry

    lax.fori_loop(0, n_chunks, score_chunk, 0)

    def column_count(hit_fn):
        def body(c, cnt):
            hit = jnp.where(hit_fn(c, s_ref[chunk_rows(c), :]), 1, 0)
            return cnt + jnp.sum(hit.reshape(TQ // SUBLANES, SUBLANES, TQ), axis=0)

        cnt = lax.fori_loop(0, n_chunks, body, jnp.zeros((SUBLANES, TQ), jnp.int32))
        return jnp.sum(cnt, axis=0, keepdims=True)

    one16, zero16 = jnp.ones((), jnp.int16), jnp.zeros((), jnp.int16)

    never = jnp.full((TQ, TQ), -HALF16, jnp.int16)
    for extra in range(COUNT_CHUNKS - 1):
        hi_ref[chunk_rows(n_chunks + extra), :] = never
        lo_ref[chunk_rows(n_chunks + extra), :] = never
    n_count_steps = (n_chunks + COUNT_CHUNKS - 1) // COUNT_CHUNKS
    count_rows = COUNT_CHUNKS * TQ

    def search16(ref, t_cnt):
        def count_ge(cand):
            def body(u, cnt):
                for part in range(COUNT_CHUNKS):
                    rows = pl.ds(pl.multiple_of(u * count_rows + part * TQ, TQ), TQ)
                    hit = jnp.where(ref[rows, :] >= cand, one16, zero16)
                    for r in range(TQ // BF16_ROWS):
                        cnt = cnt + hit[r * BF16_ROWS:(r + 1) * BF16_ROWS]
                return cnt

            cnt = lax.fori_loop(0, n_count_steps, body, jnp.zeros((BF16_ROWS, TQ), jnp.int16))
            return jnp.sum(cnt.astype(jnp.int32), axis=0, keepdims=True)

        def step(i, carry):
            t, tc = carry
            cand = t + lax.shift_left(jnp.int32(1), 15 - i)
            cnt = count_ge(cand.astype(jnp.int16))
            ok = cnt >= TOPK
            return jnp.where(ok, cand, t), jnp.where(ok, cnt, tc)

        return lax.fori_loop(0, 16, step, (jnp.full((1, TQ), -HALF16, jnp.int32), t_cnt))

    t_hi, t_cnt = search16(hi_ref, jnp.full((1, TQ), 1 << 30, jnp.int32))
    t_hi16 = t_hi.astype(jnp.int16)

    def narrow(c, carry):
        hi = hi_ref[chunk_rows(c), :]
        lo_ref[chunk_rows(c), :] = jnp.where(
            hi == t_hi16, lo_ref[chunk_rows(c), :],
            jnp.where(hi > t_hi16, jnp.full((), HALF16 - 1, jnp.int16), jnp.full((), -HALF16, jnp.int16)))
        return carry

    lax.fori_loop(0, n_chunks, narrow, 0)
    t_lo, t_cnt = search16(lo_ref, t_cnt)
    t_key = t_hi * (2 * HALF16) + (t_lo + HALF16)
    has_k = t_key > KEY_NEG_INF
    thr = _key_to_float(jnp.maximum(t_key, KEY_NEG_INF + 1))

    tied = jnp.logical_and(has_k, t_cnt > TOPK)

    @pl.when(jnp.max(jnp.where(tied, 1, 0)) > 0)
    def _():
        def count_eq_below(pos):
            return column_count(
                lambda c, blk: jnp.logical_and(blk == thr, kpos0 + c * TQ < pos))

        n_eq = count_eq_below(jnp.full((1, TQ), 1 << 30, jnp.int32))
        need = jnp.where(tied, TOPK - (t_cnt - n_eq), 1 << 30)

        def psearch(i, pos):
            cand = pos + lax.shift_left(jnp.int32(1), 12 - i)
            return jnp.where(count_eq_below(cand) <= need - 1, cand, pos)

        pmax = lax.fori_loop(0, 13, psearch, jnp.zeros((1, TQ), jnp.int32))

        def drop(c, carry):
            s = s_ref[chunk_rows(c), :]
            s_ref[chunk_rows(c), :] = jnp.where(
                jnp.logical_and(s == thr, kpos0 + c * TQ > pmax), NEG_INF, s)
            return carry

        lax.fori_loop(0, n_chunks, drop, 0)

    def to_mask(c, carry):
        s_ref[chunk_rows(c), :] = jnp.where(s_ref[chunk_rows(c), :] >= thr, 0.0, NEG_INF)
        return carry

    lax.fori_loop(0, n_chunks, to_mask, 0)

    m_ref[...] = jnp.full(m_ref.shape, NEG_INF, f32)
    acc_ref[...] = jnp.zeros(acc_ref.shape, f32)
    def attend(c, n, bias_tile):
        rows = pl.ds(pl.multiple_of(c * TQ, TQ), n * TQ)
        mask = s_ref[rows, :]
        stats = []
        for h in range(N_HEADS):
            pair = (h // 2) * LANES
            lg = _dot_nt(k_ref[0, rows, pair:pair + LANES], qp_ref[h]) + mask
            if bias_tile is not None:
                lg = lg + bias_ref[bias_tile, h]
            lg_ref[h, 0:n * TQ, :] = lg
            m_old = m_ref[h]
            m_new = jnp.maximum(m_old, jnp.max(lg, axis=0, keepdims=True))
            m_safe = jnp.where(m_new == NEG_INF, 0.0, m_new)
            m_ref[h] = m_new
            stats.append((m_safe, jnp.exp2(m_old - m_safe)))
        ones = jnp.ones((BF16_ROWS, n * TQ), bf16)
        for h in range(N_HEADS):
            m_safe, alpha = stats[h]
            p = jnp.exp2(lg_ref[h, 0:n * TQ, :] - m_safe).astype(bf16)
            va = jnp.concatenate([vt_ref[0, h * HEAD_DIM:(h + 1) * HEAD_DIM, rows], ones], axis=0)
            acc_ref[h] = alpha * acc_ref[h] + _dot(va, p)

    n_far = jnp.maximum(j - 1, 0)

    def far_pair(u, carry):
        attend(2 * u, FAR_CHUNKS, None)
        return carry

    lax.fori_loop(0, n_far // FAR_CHUNKS, far_pair, 0)

    @pl.when(n_far % FAR_CHUNKS == 1)
    def _():
        attend(n_far - 1, 1, None)

    @pl.when(j >= 1)
    def _():
        attend(j - 1, 1, 1)

    attend(j, 1, 0)

    for pair in range(N_HEADS // 2):
        even, odd = acc_ref[2 * pair], acc_ref[2 * pair + 1]
        out_t = jnp.concatenate([even[0:HEAD_DIM] / even[HEAD_DIM:HEAD_DIM + 1],
                                 odd[0:HEAD_DIM] / odd[HEAD_DIM:HEAD_DIM + 1]], axis=0)
        o_ref[0, :, pair * LANES:(pair + 1) * LANES] = out_t.T


def _dsa_prompt(q, iqb, ikw, kb, vt, ik2, bias):
    bsz, s, _ = q.shape
    once = pl.Buffered(1)
    qblk = lambda width: pl.BlockSpec((1, TQ, width), lambda b, j: (b, j, 0))
    return pl.pallas_call(
        _dsa_prompt_kernel,
        grid=(bsz, s // TQ),
        in_specs=[qblk(ATTN_W), qblk(IDX_HEADS * IDX_DIM), qblk(LANES),
                  pl.BlockSpec((1, s, ATTN_W), lambda b, j: (b, 0, 0), pipeline_mode=once),
                  pl.BlockSpec((1, ATTN_W, s), lambda b, j: (b, 0, 0), pipeline_mode=once),
                  pl.BlockSpec((1, s, LANES), lambda b, j: (b, 0, 0), pipeline_mode=once),
                  _whole(bias.shape)],
        out_specs=qblk(ATTN_W),
        out_shape=jax.ShapeDtypeStruct((bsz, s, ATTN_W), f32),
        scratch_shapes=[pltpu.VMEM((s, TQ), f32),
                        pltpu.VMEM((s + (COUNT_CHUNKS - 1) * TQ, TQ), jnp.int16),
                        pltpu.VMEM((s + (COUNT_CHUNKS - 1) * TQ, TQ), jnp.int16),
                        pltpu.VMEM((N_HEADS, TQ, LANES), bf16),
                        pltpu.VMEM((IDX_HEADS, TQ, LANES), bf16),
                        pltpu.VMEM((N_HEADS, FAR_CHUNKS * TQ, TQ), f32),
                        pltpu.VMEM((N_HEADS, 1, TQ), f32),
                        pltpu.VMEM((N_HEADS, ACC_ROWS, TQ), f32)],
        compiler_params=_params(("parallel", "arbitrary")),
        name="dsa_prompt",
    )(q, iqb, ikw, kb, vt, ik2, bias)


N_PAGES = PAST_LEN // PAGE_SIZE
N_KV_GROUPS = N_PAGES // KV_GROUP_PAGES
S_KEYS = PAST_LEN + NEW_PAD


def _softmax_max(lg, m_ref, h):
    m_old = m_ref[h]
    m_new = jnp.maximum(m_old, jnp.max(lg, axis=-1, keepdims=True))
    m_safe = jnp.where(m_new == NEG_INF, 0.0, m_new)
    m_ref[h] = m_new
    return m_safe, jnp.exp2(m_old - m_safe)


def _softmax_acc(lg, m_safe, alpha, l_ref, acc_ref, h, vt):
    p = jnp.exp2(lg - m_safe[:, 0:1])
    l_ref[h] = alpha * l_ref[h] + jnp.sum(p, axis=-1, keepdims=True)
    acc_ref[h] = alpha[:, 0:HEAD_DIM] * acc_ref[h] + _dot_nt(p.astype(bf16), vt)


def _dsa_sample_kernel(layer, pt_ref, q_ref, iq_ref, ikw_ref, knt_ref, vnt_ref, iknt_ref,
                       bias_ref, cik_ref, ck_ref, cv_ref, o_ref,
                       ikbuf, kbuf, vbuf, s_ref, m_ref, l_ref, acc_ref, sem_ik, sem_kv):
    b = pl.program_id(0)
    tp = SAMPLE_T_PAD

    def page_lanes(i):
        return pl.ds(pl.multiple_of(i * PAGE_SIZE, PAGE_SIZE), PAGE_SIZE)

    def ik_copy(p, phys):
        return pltpu.make_async_copy(cik_ref.at[layer, phys], ikbuf.at[:, page_lanes(p)], sem_ik.at[0])

    def kv_copies(slot, i, phys):
        return (pltpu.make_async_copy(ck_ref.at[layer, phys], kbuf.at[slot, :, :, page_lanes(i)], sem_kv.at[slot]),
                pltpu.make_async_copy(cv_ref.at[layer, phys], vbuf.at[slot, :, :, page_lanes(i)], sem_kv.at[slot]))

    def start_group(g, carry):
        def body(i, c):
            for cp in kv_copies(g, i, pt_ref[b, g * KV_GROUP_PAGES + i]):
                cp.start()
            return c
        return lax.fori_loop(0, KV_GROUP_PAGES, body, carry)

    def wait_group(slot):
        def body(i, carry):
            for cp in kv_copies(slot, i, 0):
                cp.wait()
            return carry
        lax.fori_loop(0, KV_GROUP_PAGES, body, 0)

    def start_ik(p, carry):
        ik_copy(p, pt_ref[b, p]).start()
        return carry

    def wait_ik(p, carry):
        ik_copy(p, 0).wait()
        return carry

    lax.fori_loop(0, N_PAGES, start_ik, 0)
    lax.fori_loop(0, N_KV_GROUPS, start_group, 0)
    lax.fori_loop(0, N_PAGES, wait_ik, 0)

    ikw = ikw_ref[0]
    row = lax.broadcasted_iota(jnp.int32, (tp, NEW_PAD), 0)
    col = lax.broadcasted_iota(jnp.int32, (tp, NEW_PAD), 1)
    for g in range(N_KV_GROUPS):
        ikc = ikbuf[:, g * KV_GROUP:(g + 1) * KV_GROUP].astype(bf16)
        s = jnp.zeros((tp, KV_GROUP), f32)
        for h in range(IDX_HEADS):
            d = _dot(iq_ref[0, h], ikc)
            s = s + jnp.maximum(d, 0.0) * (ikw[:, IDX_DIM + h:IDX_DIM + h + 1] * IDX_SCALE)
        s_ref[:, g * KV_GROUP:(g + 1) * KV_GROUP] = s
    s = jnp.zeros((tp, NEW_PAD), f32)
    for h in range(IDX_HEADS):
        d = _dot(iq_ref[0, h], iknt_ref[0])
        s = s + jnp.maximum(d, 0.0) * (ikw[:, IDX_DIM + h:IDX_DIM + h + 1] * IDX_SCALE)
    s_ref[:, PAST_LEN:] = jnp.where(col <= row, s, NEG_INF)

    def count_ge(thr):
        hit = jnp.where(s_ref[...] >= thr, 1, 0)
        return jnp.sum(hit, axis=-1, keepdims=True)

    def search(i, carry):
        t_key, t_cnt = carry
        cand = t_key + lax.shift_left(jnp.int32(1), 31 - i)
        cnt = count_ge(_key_to_float(cand))
        ok = cnt >= TOPK
        return jnp.where(ok, cand, t_key), jnp.where(ok, cnt, t_cnt)

    t_key, t_cnt = lax.fori_loop(
        0, 32, search,
        (jnp.full((tp, 1), INT_MIN, jnp.int32), jnp.full((tp, 1), 1 << 30, jnp.int32)))
    has_k = t_key > KEY_NEG_INF
    thr = _key_to_float(jnp.maximum(t_key, KEY_NEG_INF + 1))
    tied = jnp.logical_and(has_k, t_cnt > TOPK)

    @pl.when(jnp.max(jnp.where(tied, 1, 0)) > 0)
    def _():
        kpos = lax.broadcasted_iota(jnp.int32, (tp, S_KEYS), 1)

        def count_eq_below(pos):
            hit = jnp.where(jnp.logical_and(s_ref[...] == thr, kpos < pos), 1, 0)
            return jnp.sum(hit, axis=-1, keepdims=True)

        n_eq = count_eq_below(jnp.full((tp, 1), 1 << 30, jnp.int32))
        need = jnp.where(tied, TOPK - (t_cnt - n_eq), 1 << 30)

        def psearch(i, pos):
            cand = pos + lax.shift_left(jnp.int32(1), 13 - i)
            return jnp.where(count_eq_below(cand) <= need - 1, cand, pos)

        pmax = lax.fori_loop(0, 14, psearch, jnp.zeros((tp, 1), jnp.int32))
        s = s_ref[...]
        s_ref[...] = jnp.where(jnp.logical_and(s == thr, kpos > pmax), NEG_INF, s)

    m_ref[...] = jnp.full(m_ref.shape, NEG_INF, f32)
    l_ref[...] = jnp.zeros(l_ref.shape, f32)
    acc_ref[...] = jnp.zeros(acc_ref.shape, f32)

    def group(g, carry):
        slot = g
        wait_group(slot)
        ks = pl.multiple_of(g * KV_GROUP, KV_GROUP)
        sel = s_ref[:, pl.ds(ks, KV_GROUP)] >= thr
        near = jnp.where(g == N_KV_GROUPS - 1, 1.0, 0.0)
        logits, stats = [], []
        for h in range(N_HEADS):
            qh = (q_ref[0, h] * QK_SCALE).astype(bf16)
            lg = _dot(qh, kbuf[slot, h].astype(bf16)) + near * bias_ref[0, h]
            lg = jnp.where(sel, lg, NEG_INF)
            logits.append(lg)
            stats.append(_softmax_max(lg, m_ref, h))
        for h in range(N_HEADS):
            _softmax_acc(logits[h], *stats[h], l_ref, acc_ref, h, vbuf[slot, h].astype(bf16))
        return carry

    lax.fori_loop(0, N_KV_GROUPS, group, 0)

    sel = s_ref[:, PAST_LEN:] >= thr
    logits, stats = [], []
    for h in range(N_HEADS):
        qh = (q_ref[0, h] * QK_SCALE).astype(bf16)
        lg = _dot(qh, knt_ref[0, h]) + bias_ref[1, h, :, 0:NEW_PAD]
        lg = jnp.where(sel, lg, NEG_INF)
        logits.append(lg)
        stats.append(_softmax_max(lg, m_ref, h))
    for h in range(N_HEADS):
        _softmax_acc(logits[h], *stats[h], l_ref, acc_ref, h, vnt_ref[0, h])
        o_ref[0, :, h * HEAD_DIM:(h + 1) * HEAD_DIM] = acc_ref[h] / l_ref[h][:, 0:HEAD_DIM]


def _dsa_sample(layer, page_table, qh, iqh, ikw, knt, vnt, iknt, bias, cik_t, ck_t, cv_t):
    bsz = qh.shape[0]
    tp = SAMPLE_T_PAD
    any_spec = pl.BlockSpec(memory_space=pl.ANY)
    grid_spec = pltpu.PrefetchScalarGridSpec(
        num_scalar_prefetch=1,
        grid=(bsz,),
        in_specs=[pl.BlockSpec((1, N_HEADS, tp, HEAD_DIM), lambda b, pt: (b, 0, 0, 0)),
                  pl.BlockSpec((1, IDX_HEADS, tp, IDX_DIM), lambda b, pt: (b, 0, 0, 0)),
                  pl.BlockSpec((1, tp, LANES), lambda b, pt: (b, 0, 0)),
                  pl.BlockSpec((1, N_HEADS, HEAD_DIM, NEW_PAD), lambda b, pt: (b, 0, 0, 0)),
                  pl.BlockSpec((1, N_HEADS, HEAD_DIM, NEW_PAD), lambda b, pt: (b, 0, 0, 0)),
                  pl.BlockSpec((1, IDX_DIM, NEW_PAD), lambda b, pt: (b, 0, 0)),
                  pl.BlockSpec(bias.shape, lambda b, pt: (0, 0, 0, 0)),
                  any_spec, any_spec, any_spec],
        out_specs=pl.BlockSpec((1, tp, ATTN_W), lambda b, pt: (b, 0, 0)),
        scratch_shapes=[pltpu.VMEM((IDX_DIM, PAST_LEN), f32),
                        pltpu.VMEM((N_KV_GROUPS, N_HEADS, HEAD_DIM, KV_GROUP), f32),
                        pltpu.VMEM((N_KV_GROUPS, N_HEADS, HEAD_DIM, KV_GROUP), f32),
                        pltpu.VMEM((tp, S_KEYS), f32),
                        pltpu.VMEM((N_HEADS, tp, LANES), f32),
                        pltpu.VMEM((N_HEADS, tp, LANES), f32),
                        pltpu.VMEM((N_HEADS, tp, HEAD_DIM), f32),
                        pltpu.SemaphoreType.DMA((1,)),
                        pltpu.SemaphoreType.DMA((N_KV_GROUPS,))],
    )
    return pl.pallas_call(
        functools.partial(_dsa_sample_kernel, layer),
        grid_spec=grid_spec,
        out_shape=jax.ShapeDtypeStruct((bsz, tp, ATTN_W), f32),
        compiler_params=_params(("arbitrary",)),
        name="dsa_sample",
    )(page_table, qh, iqh, ikw, knt, vnt, iknt, bias, cik_t, ck_t, cv_t)


def _prompt_buckets():
    k = np.arange(TQ)[:, None]
    q = np.arange(TQ)[None, :]
    return np.stack([_rel_bucket_np(q - k), _rel_bucket_np(TQ + q - k)])


def _sample_buckets():
    t = np.arange(SAMPLE_T_PAD)[:, None]
    k = np.arange(KV_GROUP)[None, :]
    last = _rel_bucket_np(PAST_LEN + t - ((N_KV_GROUPS - 1) * KV_GROUP + k))
    new = np.full((SAMPLE_T_PAD, KV_GROUP), N_BUCKETS - 1, np.int32)
    new[:, :NEW_PAD] = _rel_bucket_np(t - np.arange(NEW_PAD)[None, :])
    return np.stack([last, new])


def _heads_major(x, bsz, t, heads, dim):
    return x.reshape(bsz, t, heads, dim).transpose(0, 2, 1, 3)


def kernel(x_prompt, x_sample, mem_prompt, cache_k, cache_v, cache_idx_k, state_conv, cache_mem_k, cache_mem_v, page_table, ln_g, ln_b, w_ff_gate, w_ff_up, w_ff_down, w_in, conv_dw, conv_db, conv_ln_g, conv_ln_b, w_mem_kv, w_br_attn, w_br_conv, w_br_mem, w_o, rel_bias):
    bp, sp, _ = x_prompt.shape
    bs, ts, _ = x_sample.shape
    depth = w_in.shape[0]

    wg = w_ff_gate.astype(bf16).reshape(depth, 2, D_MODEL, N_FF_CHUNKS, FF_CHUNK).transpose(0, 1, 3, 2, 4)
    wu = w_ff_up.astype(bf16).reshape(depth, 2, D_MODEL, N_FF_CHUNKS, FF_CHUNK).transpose(0, 1, 3, 2, 4)
    wgu = jnp.concatenate([wg, wu], axis=-1)
    wd = w_ff_down.astype(bf16).reshape(depth, 2, N_FF_CHUNKS, FF_CHUNK, D_MODEL)
    w_inb = w_in.astype(bf16)
    edges = np.cumsum([0, ATTN_W, ATTN_W, ATTN_W, IDX_HEADS * IDX_DIM, IDX_DIM + IDX_HEADS,
                       CONV_CH, CONV_CH, MEM_HEADS * MEM_HEAD_DIM, D_MODEL, D_MODEL, D_MODEL])

    nat = lambda *dts: (False, dts)
    fm = lambda *dts: (True, dts)

    def project(x, l, seq, tm, prompt):
        wq, wk, wv, wiq, wikw, wca, wcb_, wmq, wg0, wg1, wg2 = (
            w_inb[l, :, edges[i]:edges[i + 1]] for i in range(len(edges) - 1))
        w_ik = wikw[:, :IDX_DIM]
        wikw = jnp.pad(wikw, ((0, 0), (0, LANES - wikw.shape[1])))
        names = ["q", "iqb", "ikw", "cua", "cub", "mq"]
        ws = [wq, wiq, wikw, wca, wcb_, wmq]
        plan = [nat(f32), nat(bf16), nat(f32)] + [nat(f32)] * 3
        if prompt:
            names += ["kb", "kt", "vt", "vtb", "ik2"]
            ws += [wk, wk.T, wv.T, jnp.concatenate([w_ik, w_ik], axis=1)]
            plan += [nat(bf16), fm(f32), fm(f32, bf16), nat(bf16)]
        else:
            names += ["k", "v"]
            ws += [wk, wv]
            plan += [nat(f32), nat(f32)]
        return dict(zip(names, _proj(x, ws, tuple(plan), tm, seq)))

    w_memb = w_mem_kv.astype(bf16)
    wab, wcb, wmb, wob = (w.astype(bf16) for w in (w_br_attn, w_br_conv, w_br_mem, w_o))
    lng = ln_g.reshape(depth, 3, 1, D_MODEL)
    lnb = ln_b.reshape(depth, 3, 1, D_MODEL)

    bias_p = _bias_tiles(rel_bias, jnp.asarray(_prompt_buckets()))
    bias_s = _bias_tiles(rel_bias, jnp.asarray(_sample_buckets()))

    ck_t = cache_k.transpose(0, 1, 3, 4, 2)
    cv_t = cache_v.transpose(0, 1, 3, 4, 2)
    cik_t = cache_idx_k.transpose(0, 1, 3, 2)

    def layer(x, l, bsz, t, tm, prompt, mix):
        x = _ffn_ln(x, wgu[l, 0], wd[l, 0], lng[l, 0], lnb[l, 0], tm)
        p = project(x, l, t if prompt else x.shape[0], min(tm, 256), prompt)
        a, c, m, state = mix(p)
        w_gates = jnp.stack([w_inb[l, :, edges[i]:edges[i + 1]] for i in (8, 9, 10)])
        x = _merge_ln(x, a, c, m, w_gates, wab[l], wcb[l], wmb[l], wob[l],
                      lng[l, 1], lnb[l, 1], min(tm, 256))
        x = _ffn_ln(x, wgu[l, 1], wd[l, 1], lng[l, 2], lnb[l, 2], tm)
        return x, state

    def conv_args(l):
        return (conv_dw[l], conv_db[l].reshape(1, CONV_CH), conv_ln_g[l].reshape(1, CONV_CH),
                conv_ln_b[l].reshape(1, CONV_CH))

    def prompt_mix(l):
        def mix(p):
            ikw = p["ikw"]
            a = _dsa_prompt(p["q"].reshape(bp, sp, ATTN_W), p["iqb"].reshape(bp, sp, IDX_HEADS * IDX_DIM),
                            ikw.reshape(bp, sp, LANES), p["kb"].reshape(bp, sp, ATTN_W), p["vtb"],
                            p["ik2"].reshape(bp, sp, LANES), bias_p)
            prev = jnp.zeros((bp, STATE_ROWS, CONV_CH), f32)
            c, cst = _conv_module(p["cua"].reshape(bp, sp, CONV_CH), p["cub"].reshape(bp, sp, CONV_CH),
                                  prev, *conv_args(l), 512)
            mem = mem_prompt.reshape(bp * MEM_LEN, D_MODEL)
            mk, mv = _proj(mem, [w_memb[l, :, :ATTN_W], w_memb[l, :, ATTN_W:]], (nat(f32), nat(f32)),
                           256, bp * MEM_LEN)
            m = _mem_attn(p["mq"].reshape(bp, sp, ATTN_W), mk.reshape(bp, MEM_LEN, ATTN_W),
                          mv.reshape(bp, MEM_LEN, ATTN_W), 512)
            state = (p["kt"], p["vt"], ikw[:, :IDX_DIM].reshape(bp, sp, IDX_DIM), cst[:, STATE_PAD:],
                     mk.reshape(bp, MEM_LEN, MEM_HEADS, MEM_HEAD_DIM),
                     mv.reshape(bp, MEM_LEN, MEM_HEADS, MEM_HEAD_DIM))
            return (a.reshape(bp * sp, ATTN_W), c.reshape(bp * sp, CONV_CH),
                    m.reshape(bp * sp, ATTN_W), state)
        return mix

    def pad_to(x, axis, size):
        pad = [(0, 0)] * x.ndim
        pad[axis] = (0, size - x.shape[axis])
        return jnp.pad(x, pad)

    def sample_mix(l):
        def mix(p):
            q, k, v, iqb, ikw, cua, cub, mq = (p[n] for n in ("q", "k", "v", "iqb", "ikw", "cua", "cub", "mq"))
            tp = SAMPLE_T_PAD
            qh = pad_to(_heads_major(q, bs, ts, N_HEADS, HEAD_DIM), 2, tp)
            iqh = pad_to(_heads_major(iqb, bs, ts, IDX_HEADS, IDX_DIM), 2, tp)
            knt = pad_to(k.reshape(bs, ts, N_HEADS, HEAD_DIM).transpose(0, 2, 3, 1), 3, NEW_PAD).astype(bf16)
            vnt = pad_to(v.reshape(bs, ts, N_HEADS, HEAD_DIM).transpose(0, 2, 3, 1), 3, NEW_PAD).astype(bf16)
            iknt = pad_to(ikw[:, :IDX_DIM].reshape(bs, ts, IDX_DIM).transpose(0, 2, 1), 2, NEW_PAD).astype(bf16)
            ikw3 = pad_to(ikw.reshape(bs, ts, LANES), 1, tp)
            a = _dsa_sample(l, page_table, qh, iqh, ikw3, knt, vnt, iknt, bias_s,
                            cik_t, ck_t, cv_t)[:, :ts]
            prev = jnp.pad(state_conv[l], ((0, 0), (STATE_PAD, 0), (0, 0)))
            c, cst = _conv_module(cua.reshape(bs, ts, CONV_CH), cub.reshape(bs, ts, CONV_CH), prev,
                                  *conv_args(l), ts)
            mqp = pad_to(mq.reshape(bs, ts, ATTN_W), 1, tp)
            m = _mem_attn(mqp, cache_mem_k[l].reshape(bs, MEM_LEN, ATTN_W),
                          cache_mem_v[l].reshape(bs, MEM_LEN, ATTN_W), tp)[:, :ts]
            state = (k.reshape(bs, ts, N_HEADS, HEAD_DIM), v.reshape(bs, ts, N_HEADS, HEAD_DIM),
                     ikw[:, :IDX_DIM].reshape(bs, ts, IDX_DIM), cst[:, STATE_PAD:])
            return (a.reshape(bs * ts, ATTN_W), c.reshape(bs * ts, CONV_CH),
                    m.reshape(bs * ts, ATTN_W), state)
        return mix

    xp = x_prompt.reshape(bp * sp, D_MODEL)
    xs = x_sample.reshape(bs * ts, D_MODEL)
    st_p, st_s = [], []
    for l in range(depth):
        xp, s_p = layer(xp, l, bp, sp, 1024, True, prompt_mix(l))
        xs, s_s = layer(xs, l, bs, ts, bs * ts, False, sample_mix(l))
        st_p.append(s_p)
        st_s.append(s_s)

    outs_p = [jnp.stack([s[i] for s in st_p]) for i in range(6)]
    for i in (0, 1):
        outs_p[i] = outs_p[i].reshape(depth, bp, N_HEADS, HEAD_DIM, sp).transpose(0, 1, 4, 2, 3)
    outs_s = [jnp.stack([s[i] for s in st_s]) for i in range(4)]
    return (xp.reshape(bp, sp, D_MODEL), xs.reshape(bs, ts, D_MODEL), *outs_p, *outs_s)
```

```python
import functools
import math

import numpy as np
import jax
import jax.numpy as jnp
from jax import lax
from jax.experimental import pallas as pl
from jax.experimental.pallas import tpu as pltpu

D_MODEL = 1024
N_HEADS = 8
HEAD_DIM = 64
ATTN_W = 512
IDX_HEADS = 8
IDX_DIM = 64
TOPK = 256
CONV_CH = 512
CONV_WIDTH = 31
MEM_LEN = 256
MEM_HEADS = 4
MEM_HEAD_DIM = 128
D_FF = 2816
N_BUCKETS = 32
MAX_DISTANCE = 128
LN_EPS = 1e-5
DEPTH = 2
ALPHA = (2 * DEPTH) ** 0.25
PAGE_SIZE = 128
PAST_LEN = 8192

LANES = 128
SUBLANES = 8
BF16_ROWS = 16
VMEM_LIMIT = 56 * 1024 * 1024

FF_CHUNK = 256
N_FF_CHUNKS = D_FF // FF_CHUNK
TQ = 256
FAR_CHUNKS = 2
COUNT_CHUNKS = 2
STATE_ROWS = 32
STATE_PAD = STATE_ROWS - (CONV_WIDTH - 1)
SAMPLE_T_PAD = 8
NEW_PAD = 128
KV_GROUP_PAGES = 16
KV_GROUP = KV_GROUP_PAGES * PAGE_SIZE

IDX_SCALE = (IDX_DIM ** -0.5) * (IDX_HEADS ** -0.5)
LOG2E = math.log2(math.e)
QK_SCALE = (HEAD_DIM ** -0.5) * LOG2E
INT_MIN = -(2 ** 31)
HALF16 = 2 ** 15
KEY_NEG_INF = -2139095041
NEG_INF = float("-inf")

bf16 = jnp.bfloat16
f32 = jnp.float32


def _dot(a, b):
    return jnp.dot(a, b, preferred_element_type=f32)


def _dot_nt(a, b):
    return lax.dot_general(a, b, (((1,), (1,)), ((), ())), preferred_element_type=f32)


def _sigmoid(x):
    return 1.0 / (1.0 + jnp.exp(-x))


def _layer_norm(y, g, b):
    mu = jnp.mean(y, axis=-1, keepdims=True)
    d = y - mu
    var = jnp.mean(d * d, axis=-1, keepdims=True)
    return d * lax.rsqrt(var + LN_EPS) * g + b


def _key_to_float(key):
    bits = key ^ (lax.shift_right_arithmetic(key, 31) & 0x7FFFFFFF)
    return lax.bitcast_convert_type(bits, f32)


def _params(sem):
    return pltpu.CompilerParams(dimension_semantics=sem, vmem_limit_bytes=VMEM_LIMIT)


def _whole(shape):
    nd = len(shape)
    return pl.BlockSpec(shape, lambda *_: (0,) * nd)


LN_ROWS = 128


def _ffn_ln_kernel(x_ref, wgu_ref, wd_ref, g_ref, b_ref, o_ref, xb_ref, acc_ref):
    xb_ref[...] = x_ref[...].astype(bf16)
    acc_ref[...] = jnp.zeros_like(acc_ref)

    def chunk(c, carry):
        gu = _dot(xb_ref[...], wgu_ref[c])
        gate, up = gu[:, :FF_CHUNK], gu[:, FF_CHUNK:]
        h = (gate * _sigmoid(gate) * up).astype(bf16)
        acc_ref[...] += _dot(h, wd_ref[c])
        return carry

    lax.fori_loop(0, N_FF_CHUNKS, chunk, 0)

    def post_norm(r, carry):
        rows = pl.ds(pl.multiple_of(r * LN_ROWS, LN_ROWS), LN_ROWS)
        y = ALPHA * x_ref[rows, :] + 0.5 * acc_ref[rows, :]
        o_ref[rows, :] = _layer_norm(y, g_ref[...], b_ref[...])
        return carry

    lax.fori_loop(0, x_ref.shape[0] // LN_ROWS, post_norm, 0)


def _ffn_ln(x, wgu, wd, g, b, tm):
    n = x.shape[0]
    return pl.pallas_call(
        _ffn_ln_kernel,
        grid=(n // tm,),
        in_specs=[pl.BlockSpec((tm, D_MODEL), lambda i: (i, 0)),
                  _whole(wgu.shape), _whole(wd.shape), _whole(g.shape), _whole(b.shape)],
        out_specs=pl.BlockSpec((tm, D_MODEL), lambda i: (i, 0)),
        out_shape=jax.ShapeDtypeStruct((n, D_MODEL), f32),
        scratch_shapes=[pltpu.VMEM((tm, D_MODEL), bf16), pltpu.VMEM((tm, D_MODEL), f32)],
        compiler_params=_params(("parallel",)),
        name="ffn_ln",
    )(x, wgu, wd, g, b)


def _proj_kernel(plan, x_ref, *refs):
    xb = x_ref[...].astype(bf16)
    o_refs = iter(refs[len(plan):])
    for w_ref, (feature_major, dts) in zip(refs[:len(plan)], plan):
        r = _dot_nt(w_ref[...], xb) if feature_major else _dot(xb, w_ref[...])
        for dt in dts:
            o_ref = next(o_refs)
            o_ref[...] = r.astype(dt).reshape(o_ref.shape)


def _proj(x, ws, plan, tm, seq):
    n = x.shape[0]
    per_seq = seq // tm
    specs, shapes = [], []
    for w, (feature_major, dts) in zip(ws, plan):
        for dt in dts:
            if feature_major:
                specs.append(pl.BlockSpec((1, w.shape[0], tm), lambda i: (i // per_seq, 0, i % per_seq)))
                shapes.append(jax.ShapeDtypeStruct((n // seq, w.shape[0], seq), dt))
            else:
                specs.append(pl.BlockSpec((tm, w.shape[1]), lambda i: (i, 0)))
                shapes.append(jax.ShapeDtypeStruct((n, w.shape[1]), dt))
    return pl.pallas_call(
        functools.partial(_proj_kernel, plan),
        grid=(n // tm,),
        in_specs=[pl.BlockSpec((tm, D_MODEL), lambda i: (i, 0))] + [_whole(w.shape) for w in ws],
        out_specs=specs,
        out_shape=shapes,
        compiler_params=_params(("parallel",)),
        name="in_proj",
    )(x, *ws)


CONV_ROWS = 64
SHIFT_TAIL = STATE_ROWS - SUBLANES


def _conv_kernel(tt, n_tiles, a_ref, b_ref, prev_ref, dw_ref, db_ref, g_ref, bb_ref,
                 c_ref, st_ref, ext_ref, sh_ref):
    j = pl.program_id(1)
    rb = min(tt, CONV_ROWS)

    @pl.when(j == 0)
    def _():
        ext_ref[0:STATE_ROWS, :] = prev_ref[0]

    ext_ref[STATE_ROWS:STATE_ROWS + tt, :] = a_ref[0] * _sigmoid(b_ref[0])
    for r in range(1, SUBLANES):
        sh_ref[r - 1] = ext_ref[r:r + tt + SHIFT_TAIL, :]

    def block(i, carry):
        base = i * rb
        acc = jnp.zeros((rb, CONV_CH), f32)
        for tap in range(CONV_WIDTH):
            off = STATE_PAD + tap
            rows = pl.ds(pl.multiple_of(base + off // SUBLANES * SUBLANES, min(rb, SUBLANES)), rb)
            src = ext_ref[rows, :] if off % SUBLANES == 0 else sh_ref[off % SUBLANES - 1, rows, :]
            acc = acc + src * dw_ref[tap:tap + 1, :]
        c_ref[0, pl.ds(pl.multiple_of(base, rb), rb), :] = acc
        return carry

    lax.fori_loop(0, tt // rb, block, 0)
    y = _layer_norm(c_ref[0] + db_ref[...], g_ref[...], bb_ref[...])
    c_ref[0] = y * _sigmoid(y)
    tail = ext_ref[tt:tt + STATE_ROWS, :]
    st_ref[0] = tail
    if n_tiles > 1:
        ext_ref[0:STATE_ROWS, :] = tail


def _conv_module(a, b, prev, dw, db, g, bb, tt):
    bsz, t, _ = a.shape
    n_tiles = t // tt
    blk = pl.BlockSpec((1, tt, CONV_CH), lambda i, j: (i, j, 0))
    st = pl.BlockSpec((1, STATE_ROWS, CONV_CH), lambda i, j: (i, 0, 0))
    return pl.pallas_call(
        functools.partial(_conv_kernel, tt, n_tiles),
        grid=(bsz, n_tiles),
        in_specs=[blk, blk, st, _whole(dw.shape), _whole(db.shape), _whole(g.shape), _whole(bb.shape)],
        out_specs=[blk, st],
        out_shape=[jax.ShapeDtypeStruct((bsz, t, CONV_CH), f32),
                   jax.ShapeDtypeStruct((bsz, STATE_ROWS, CONV_CH), f32)],
        scratch_shapes=[pltpu.VMEM((STATE_ROWS + tt, CONV_CH), f32),
                        pltpu.VMEM((SUBLANES - 1, tt + SHIFT_TAIL, CONV_CH), f32)],
        compiler_params=_params(("parallel", "arbitrary")),
        name="conv_module",
    )(a, b, prev, dw, db, g, bb)


def _mem_attn_kernel(q_ref, mk_ref, mv_ref, o_ref):
    scale = MEM_HEAD_DIM ** -0.5
    for h in range(MEM_HEADS):
        sl = slice(h * MEM_HEAD_DIM, (h + 1) * MEM_HEAD_DIM)
        qh = (q_ref[0, :, sl] * scale).astype(bf16)
        kh = mk_ref[0, :, sl].astype(bf16)
        vh = mv_ref[0, :, sl].astype(bf16)
        lg = _dot_nt(qh, kh)
        mx = jnp.max(lg, axis=-1, keepdims=True)
        p = jnp.exp(lg - mx)
        den = jnp.sum(p, axis=-1, keepdims=True)
        o_ref[0, :, sl] = _dot((p / den).astype(bf16), vh)


def _mem_attn(q, mk, mv, tq):
    bsz, t, w = q.shape
    qblk = pl.BlockSpec((1, tq, w), lambda i, j: (i, j, 0))
    mblk = pl.BlockSpec((1, MEM_LEN, w), lambda i, j: (i, 0, 0))
    return pl.pallas_call(
        _mem_attn_kernel,
        grid=(bsz, t // tq),
        in_specs=[qblk, mblk, mblk],
        out_specs=qblk,
        out_shape=jax.ShapeDtypeStruct((bsz, t, w), f32),
        compiler_params=_params(("parallel", "parallel")),
        name="mem_attn",
    )(q, mk, mv)


def _merge_ln_kernel(x_ref, a_ref, c_ref, m_ref, wg_ref, wa_ref, wc_ref, wm_ref, wo_ref,
                     g_ref, b_ref, o_ref):
    xb = x_ref[...].astype(bf16)
    y = _sigmoid(_dot(xb, wg_ref[0])) * _dot(a_ref[...].astype(bf16), wa_ref[...])
    y = y + _sigmoid(_dot(xb, wg_ref[1])) * _dot(c_ref[...].astype(bf16), wc_ref[...])
    y = y + _sigmoid(_dot(xb, wg_ref[2])) * _dot(m_ref[...].astype(bf16), wm_ref[...])
    z = _dot(y.astype(bf16), wo_ref[...])
    o_ref[...] = _layer_norm(ALPHA * x_ref[...] + z, g_ref[...], b_ref[...])


def _merge_ln(x, a, c, m, wg, wa, wc, wm, wo, g, b, tm):
    n = x.shape[0]
    wide = pl.BlockSpec((tm, D_MODEL), lambda i: (i, 0))
    half = pl.BlockSpec((tm, ATTN_W), lambda i: (i, 0))
    return pl.pallas_call(
        _merge_ln_kernel,
        grid=(n // tm,),
        in_specs=[wide, half, half, half, _whole(wg.shape),
                  _whole(wa.shape), _whole(wc.shape), _whole(wm.shape), _whole(wo.shape),
                  _whole(g.shape), _whole(b.shape)],
        out_specs=wide,
        out_shape=jax.ShapeDtypeStruct((n, D_MODEL), f32),
        compiler_params=_params(("parallel",)),
        name="merge_ln",
    )(x, a, c, m, wg, wa, wc, wm, wo, g, b)


def _rel_bucket_np(n):
    n = np.maximum(n, 0)
    exact = N_BUCKETS // 2
    nf = np.maximum(n, 1).astype(np.float64)
    large = exact + (np.log(nf / exact) / math.log(MAX_DISTANCE / exact) * (N_BUCKETS - exact)).astype(np.int64)
    large = np.minimum(large, N_BUCKETS - 1)
    return np.where(n < exact, n, large).astype(np.int32)


def _bias_kernel(rb_ref, bkt_ref, o_ref):
    n_tiles = bkt_ref.shape[0]
    for t in range(n_tiles):
        bkt = bkt_ref[t]
        for h in range(N_HEADS):
            far = rb_ref[N_BUCKETS - 1, h]
            acc = jnp.zeros(bkt.shape, f32)
            for bucket in range(N_BUCKETS - 1):
                acc = jnp.where(bkt == bucket, (rb_ref[bucket, h] - far) * LOG2E, acc)
            o_ref[t, h] = acc


def _bias_tiles(rel_bias, buckets):
    n_tiles, r, c = buckets.shape
    return pl.pallas_call(
        _bias_kernel,
        in_specs=[pl.BlockSpec(memory_space=pltpu.SMEM), _whole(buckets.shape)],
        out_specs=_whole((n_tiles, N_HEADS, r, c)),
        out_shape=jax.ShapeDtypeStruct((n_tiles, N_HEADS, r, c), f32),
        grid=(1,),
        name="rel_bias_tiles",
    )(rel_bias, buckets)


ACC_ROWS = HEAD_DIM + BF16_ROWS


def _dsa_prompt_kernel(q_ref, iq_ref, ikw_ref, k_ref, vt_ref, ik2_ref, bias_ref, o_ref,
                       s_ref, hi_ref, lo_ref, qp_ref, iqp_ref, lg_ref, m_ref, acc_ref):
    j = pl.program_id(1)
    n_chunks = j + 1
    kpos0 = lax.broadcasted_iota(jnp.int32, (TQ, TQ), 0)
    qpos0 = lax.broadcasted_iota(jnp.int32, (TQ, TQ), 1)

    def chunk_rows(c):
        return pl.ds(pl.multiple_of(c * TQ, TQ), TQ)

    lane = lax.broadcasted_iota(jnp.int32, (TQ, LANES), 1)
    for h in range(N_HEADS):
        pair = slice((h // 2) * LANES, (h // 2 + 1) * LANES)
        mine = (lane < HEAD_DIM) if h % 2 == 0 else (lane >= HEAD_DIM)
        qp_ref[h] = jnp.where(mine, q_ref[0, :, pair] * QK_SCALE, 0.0).astype(bf16)
        iqp_ref[h] = jnp.where(mine, iq_ref[0, :, pair], jnp.zeros((), bf16))

    w = ikw_ref[0].T[IDX_DIM:IDX_DIM + IDX_HEADS, :] * IDX_SCALE

    def score_chunk(c, diagonal):
        ikc = ik2_ref[0, chunk_rows(c), :]
        s = jnp.zeros((TQ, TQ), f32)
        for h in range(IDX_HEADS):
            d = _dot_nt(ikc, iqp_ref[h])
            s = s + jnp.maximum(d, 0.0) * w[h:h + 1, :]
        s = jnp.where(s == 0.0, 0.0, s)
        if diagonal:
            s = jnp.where(kpos0 <= qpos0, s, NEG_INF)
        s_ref[chunk_rows(c), :] = s
        bits = lax.bitcast_convert_type(s, jnp.int32)
        key = bits ^ (lax.shift_right_arithmetic(bits, 31) & 0x7FFFFFFF)
        hi_ref[chunk_rows(c), :] = lax.shift_right_arithmetic(key, 16).astype(jnp.int16)
        lo_ref[chunk_rows(c), :] = ((key & 0xFFFF) - HALF16).astype(jnp.int16)

    def score_below(c, carry):
        score_chunk(c, False)
        return carry

    lax.fori_loop(0, j, score_below, 0)
    score_chunk(j, True)

    def column_count(hit_fn):
        def body(c, cnt):
            hit = jnp.where(hit_fn(c, s_ref[chunk_rows(c), :]), 1, 0)
            return cnt + jnp.sum(hit.reshape(TQ // SUBLANES, SUBLANES, TQ), axis=0)

        cnt = lax.fori_loop(0, n_chunks, body, jnp.zeros((SUBLANES, TQ), jnp.int32))
        return jnp.sum(cnt, axis=0, keepdims=True)

    one16, zero16 = jnp.ones((), jnp.int16), jnp.zeros((), jnp.int16)

    never = jnp.full((TQ, TQ), -HALF16, jnp.int16)
    for extra in range(COUNT_CHUNKS - 1):
        hi_ref[chunk_rows(n_chunks + extra), :] = never
        lo_ref[chunk_rows(n_chunks + extra), :] = never
    n_count_steps = (n_chunks + COUNT_CHUNKS - 1) // COUNT_CHUNKS
    count_rows = COUNT_CHUNKS * TQ

    def search16(ref, t_cnt):
        def count_ge(cand):
            def body(u, cnt):
                for part in range(COUNT_CHUNKS):
                    rows = pl.ds(pl.multiple_of(u * count_rows + part * TQ, TQ), TQ)
                    hit = jnp.where(ref[rows, :] >= cand, one16, zero16)
                    for r in range(TQ // BF16_ROWS):
                        cnt = cnt + hit[r * BF16_ROWS:(r + 1) * BF16_ROWS]
                return cnt

            cnt = lax.fori_loop(0, n_count_steps, body, jnp.zeros((BF16_ROWS, TQ), jnp.int16))
            return jnp.sum(cnt.astype(jnp.int32), axis=0, keepdims=True)

        def step(i, carry):
            t, tc = carry
            cand = t + lax.shift_left(jnp.int32(1), 15 - i)
            cnt = count_ge(cand.astype(jnp.int16))
            ok = cnt >= TOPK
            return jnp.where(ok, cand, t), jnp.where(ok, cnt, tc)

        return lax.fori_loop(0, 16, step, (jnp.full((1, TQ), -HALF16, jnp.int32), t_cnt))

    t_hi, t_cnt = search16(hi_ref, jnp.full((1, TQ), 1 << 30, jnp.int32))
    t_hi16 = t_hi.astype(jnp.int16)

    def narrow(c, carry):
        hi = hi_ref[chunk_rows(c), :]
        lo_ref[chunk_rows(c), :] = jnp.where(
            hi == t_hi16, lo_ref[chunk_rows(c), :],
            jnp.where(hi > t_hi16, jnp.full((), HALF16 - 1, jnp.int16), jnp.full((), -HALF16, jnp.int16)))
        return carry

    lax.fori_loop(0, n_chunks, narrow, 0)
    t_lo, t_cnt = search16(lo_ref, t_cnt)
    t_key = t_hi * (2 * HALF16) + (t_lo + HALF16)
    has_k = t_key > KEY_NEG_INF
    thr = _key_to_float(jnp.maximum(t_key, KEY_NEG_INF + 1))

    tied = jnp.logical_and(has_k, t_cnt > TOPK)

    @pl.when(jnp.max(jnp.where(tied, 1, 0)) > 0)
    def _():
        def count_eq_below(pos):
            return column_count(
                lambda c, blk: jnp.logical_and(blk == thr, kpos0 + c * TQ < pos))

        n_eq = count_eq_below(jnp.full((1, TQ), 1 << 30, jnp.int32))
        need = jnp.where(tied, TOPK - (t_cnt - n_eq), 1 << 30)

        def psearch(i, pos):
            cand = pos + lax.shift_left(jnp.int32(1), 12 - i)
            return jnp.where(count_eq_below(cand) <= need - 1, cand, pos)

        pmax = lax.fori_loop(0, 13, psearch, jnp.zeros((1, TQ), jnp.int32))

        def drop(c, carry):
            s = s_ref[chunk_rows(c), :]
            s_ref[chunk_rows(c), :] = jnp.where(
                jnp.logical_and(s == thr, kpos0 + c * TQ > pmax), NEG_INF, s)
            return carry

        lax.fori_loop(0, n_chunks, drop, 0)

    def to_mask(c, carry):
        s_ref[chunk_rows(c), :] = jnp.where(s_ref[chunk_rows(c), :] >= thr, 0.0, NEG_INF)
        return carry

    lax.fori_loop(0, n_chunks, to_mask, 0)

    m_ref[...] = jnp.full(m_ref.shape, NEG_INF, f32)
    acc_ref[...] = jnp.zeros(acc_ref.shape, f32)
    def attend(c, n, bias_tile):
        rows = pl.ds(pl.multiple_of(c * TQ, TQ), n * TQ)
        mask = s_ref[rows, :]
        stats = []
        for h in range(N_HEADS):
            pair = (h // 2) * LANES
            lg = _dot_nt(k_ref[0, rows, pair:pair + LANES], qp_ref[h]) + mask
            if bias_tile is not None:
                lg = lg + bias_ref[bias_tile, h]
            lg_ref[h, 0:n * TQ, :] = lg
            m_old = m_ref[h]
            m_new = jnp.maximum(m_old, jnp.max(lg, axis=0, keepdims=True))
            m_safe = jnp.where(m_new == NEG_INF, 0.0, m_new)
            m_ref[h] = m_new
            stats.append((m_safe, jnp.exp2(m_old - m_safe)))
        ones = jnp.ones((BF16_ROWS, n * TQ), bf16)
        for h in range(N_HEADS):
            m_safe, alpha = stats[h]
            p = jnp.exp2(lg_ref[h, 0:n * TQ, :] - m_safe).astype(bf16)
            va = jnp.concatenate([vt_ref[0, h * HEAD_DIM:(h + 1) * HEAD_DIM, rows], ones], axis=0)
            acc_ref[h] = alpha * acc_ref[h] + _dot(va, p)

    n_far = jnp.maximum(j - 1, 0)

    def far_pair(u, carry):
        attend(2 * u, FAR_CHUNKS, None)
        return carry

    lax.fori_loop(0, n_far // FAR_CHUNKS, far_pair, 0)

    @pl.when(n_far % FAR_CHUNKS == 1)
    def _():
        attend(n_far - 1, 1, None)

    @pl.when(j >= 1)
    def _():
        attend(j - 1, 1, 1)

    attend(j, 1, 0)

    for pair in range(N_HEADS // 2):
        even, odd = acc_ref[2 * pair], acc_ref[2 * pair + 1]
        out_t = jnp.concatenate([even[0:HEAD_DIM] / even[HEAD_DIM:HEAD_DIM + 1],
                                 odd[0:HEAD_DIM] / odd[HEAD_DIM:HEAD_DIM + 1]], axis=0)
        o_ref[0, :, pair * LANES:(pair + 1) * LANES] = out_t.T


def _dsa_prompt(q, iqb, ikw, kb, vt, ik2, bias):
    bsz, s, _ = q.shape
    once = pl.Buffered(1)
    qblk = lambda width: pl.BlockSpec((1, TQ, width), lambda b, j: (b, j, 0))
    return pl.pallas_call(
        _dsa_prompt_kernel,
        grid=(bsz, s // TQ),
        in_specs=[qblk(ATTN_W), qblk(IDX_HEADS * IDX_DIM), qblk(LANES),
                  pl.BlockSpec((1, s, ATTN_W), lambda b, j: (b, 0, 0), pipeline_mode=once),
                  pl.BlockSpec((1, ATTN_W, s), lambda b, j: (b, 0, 0), pipeline_mode=once),
                  pl.BlockSpec((1, s, LANES), lambda b, j: (b, 0, 0), pipeline_mode=once),
                  _whole(bias.shape)],
        out_specs=qblk(ATTN_W),
        out_shape=jax.ShapeDtypeStruct((bsz, s, ATTN_W), f32),
        scratch_shapes=[pltpu.VMEM((s, TQ), f32),
                        pltpu.VMEM((s + (COUNT_CHUNKS - 1) * TQ, TQ), jnp.int16),
                        pltpu.VMEM((s + (COUNT_CHUNKS - 1) * TQ, TQ), jnp.int16),
                        pltpu.VMEM((N_HEADS, TQ, LANES), bf16),
                        pltpu.VMEM((IDX_HEADS, TQ, LANES), bf16),
                        pltpu.VMEM((N_HEADS, FAR_CHUNKS * TQ, TQ), f32),
                        pltpu.VMEM((N_HEADS, 1, TQ), f32),
                        pltpu.VMEM((N_HEADS, ACC_ROWS, TQ), f32)],
        compiler_params=_params(("parallel", "arbitrary")),
        name="dsa_prompt",
    )(q, iqb, ikw, kb, vt, ik2, bias)


N_PAGES = PAST_LEN // PAGE_SIZE
N_KV_GROUPS = N_PAGES // KV_GROUP_PAGES
S_KEYS = PAST_LEN + NEW_PAD


def _softmax_max(lg, m_ref, h):
    m_old = m_ref[h]
    m_new = jnp.maximum(m_old, jnp.max(lg, axis=-1, keepdims=True))
    m_safe = jnp.where(m_new == NEG_INF, 0.0, m_new)
    m_ref[h] = m_new
    return m_safe, jnp.exp2(m_old - m_safe)


def _softmax_acc(lg, m_safe, alpha, l_ref, acc_ref, h, vt):
    p = jnp.exp2(lg - m_safe[:, 0:1])
    l_ref[h] = alpha * l_ref[h] + jnp.sum(p, axis=-1, keepdims=True)
    acc_ref[h] = alpha[:, 0:HEAD_DIM] * acc_ref[h] + _dot_nt(p.astype(bf16), vt)


def _dsa_sample_kernel(layer, pt_ref, q_ref, iq_ref, ikw_ref, knt_ref, vnt_ref, iknt_ref,
                       bias_ref, cik_ref, ck_ref, cv_ref, o_ref,
                       ikbuf, kbuf, vbuf, s_ref, m_ref, l_ref, acc_ref, sem_ik, sem_kv):
    b = pl.program_id(0)
    tp = SAMPLE_T_PAD

    def page_lanes(i):
        return pl.ds(pl.multiple_of(i * PAGE_SIZE, PAGE_SIZE), PAGE_SIZE)

    def ik_copy(p, phys):
        return pltpu.make_async_copy(cik_ref.at[layer, phys], ikbuf.at[:, page_lanes(p)], sem_ik.at[0])

    def kv_copies(slot, i, phys):
        return (pltpu.make_async_copy(ck_ref.at[layer, phys], kbuf.at[slot, :, :, page_lanes(i)], sem_kv.at[slot]),
                pltpu.make_async_copy(cv_ref.at[layer, phys], vbuf.at[slot, :, :, page_lanes(i)], sem_kv.at[slot]))

    def start_group(g, carry):
        def body(i, c):
            for cp in kv_copies(g, i, pt_ref[b, g * KV_GROUP_PAGES + i]):
                cp.start()
            return c
        return lax.fori_loop(0, KV_GROUP_PAGES, body, carry)

    def wait_group(slot):
        def body(i, carry):
            for cp in kv_copies(slot, i, 0):
                cp.wait()
            return carry
        lax.fori_loop(0, KV_GROUP_PAGES, body, 0)

    def start_ik(p, carry):
        ik_copy(p, pt_ref[b, p]).start()
        return carry

    def wait_ik(p, carry):
        ik_copy(p, 0).wait()
        return carry

    lax.fori_loop(0, N_PAGES, start_ik, 0)
    lax.fori_loop(0, N_KV_GROUPS, start_group, 0)
    lax.fori_loop(0, N_PAGES, wait_ik, 0)

    ikw = ikw_ref[0]
    row = lax.broadcasted_iota(jnp.int32, (tp, NEW_PAD), 0)
    col = lax.broadcasted_iota(jnp.int32, (tp, NEW_PAD), 1)
    for g in range(N_KV_GROUPS):
        ikc = ikbuf[:, g * KV_GROUP:(g + 1) * KV_GROUP].astype(bf16)
        s = jnp.zeros((tp, KV_GROUP), f32)
        for h in range(IDX_HEADS):
            d = _dot(iq_ref[0, h], ikc)
            s = s + jnp.maximum(d, 0.0) * (ikw[:, IDX_DIM + h:IDX_DIM + h + 1] * IDX_SCALE)
        s_ref[:, g * KV_GROUP:(g + 1) * KV_GROUP] = s
    s = jnp.zeros((tp, NEW_PAD), f32)
    for h in range(IDX_HEADS):
        d = _dot(iq_ref[0, h], iknt_ref[0])
        s = s + jnp.maximum(d, 0.0) * (ikw[:, IDX_DIM + h:IDX_DIM + h + 1] * IDX_SCALE)
    s_ref[:, PAST_LEN:] = jnp.where(col <= row, s, NEG_INF)

    def count_ge(thr):
        hit = jnp.where(s_ref[...] >= thr, 1, 0)
        return jnp.sum(hit, axis=-1, keepdims=True)

    def search(i, carry):
        t_key, t_cnt = carry
        cand = t_key + lax.shift_left(jnp.int32(1), 31 - i)
        cnt = count_ge(_key_to_float(cand))
        ok = cnt >= TOPK
        return jnp.where(ok, cand, t_key), jnp.where(ok, cnt, t_cnt)

    t_key, t_cnt = lax.fori_loop(
        0, 32, search,
        (jnp.full((tp, 1), INT_MIN, jnp.int32), jnp.full((tp, 1), 1 << 30, jnp.int32)))
    has_k = t_key > KEY_NEG_INF
    thr = _key_to_float(jnp.maximum(t_key, KEY_NEG_INF + 1))
    tied = jnp.logical_and(has_k, t_cnt > TOPK)

    @pl.when(jnp.max(jnp.where(tied, 1, 0)) > 0)
    def _():
        kpos = lax.broadcasted_iota(jnp.int32, (tp, S_KEYS), 1)

        def count_eq_below(pos):
            hit = jnp.where(jnp.logical_and(s_ref[...] == thr, kpos < pos), 1, 0)
            return jnp.sum(hit, axis=-1, keepdims=True)

        n_eq = count_eq_below(jnp.full((tp, 1), 1 << 30, jnp.int32))
        need = jnp.where(tied, TOPK - (t_cnt - n_eq), 1 << 30)

        def psearch(i, pos):
            cand = pos + lax.shift_left(jnp.int32(1), 13 - i)
            return jnp.where(count_eq_below(cand) <= need - 1, cand, pos)

        pmax = lax.fori_loop(0, 14, psearch, jnp.zeros((tp, 1), jnp.int32))
        s = s_ref[...]
        s_ref[...] = jnp.where(jnp.logical_and(s == thr, kpos > pmax), NEG_INF, s)

    m_ref[...] = jnp.full(m_ref.shape, NEG_INF, f32)
    l_ref[...] = jnp.zeros(l_ref.shape, f32)
    acc_ref[...] = jnp.zeros(acc_ref.shape, f32)

    def group(g, carry):
        slot = g
        wait_group(slot)
        ks = pl.multiple_of(g * KV_GROUP, KV_GROUP)
        sel = s_ref[:, pl.ds(ks, KV_GROUP)] >= thr
        near = jnp.where(g == N_KV_GROUPS - 1, 1.0, 0.0)
        logits, stats = [], []
        for h in range(N_HEADS):
            qh = (q_ref[0, h] * QK_SCALE).astype(bf16)
            lg = _dot(qh, kbuf[slot, h].astype(bf16)) + near * bias_ref[0, h]
            lg = jnp.where(sel, lg, NEG_INF)
            logits.append(lg)
            stats.append(_softmax_max(lg, m_ref, h))
        for h in range(N_HEADS):
            _softmax_acc(logits[h], *stats[h], l_ref, acc_ref, h, vbuf[slot, h].astype(bf16))
        return carry

    lax.fori_loop(0, N_KV_GROUPS, group, 0)

    sel = s_ref[:, PAST_LEN:] >= thr
    logits, stats = [], []
    for h in range(N_HEADS):
        qh = (q_ref[0, h] * QK_SCALE).astype(bf16)
        lg = _dot(qh, knt_ref[0, h]) + bias_ref[1, h, :, 0:NEW_PAD]
        lg = jnp.where(sel, lg, NEG_INF)
        logits.append(lg)
        stats.append(_softmax_max(lg, m_ref, h))
    for h in range(N_HEADS):
        _softmax_acc(logits[h], *stats[h], l_ref, acc_ref, h, vnt_ref[0, h])
        o_ref[0, :, h * HEAD_DIM:(h + 1) * HEAD_DIM] = acc_ref[h] / l_ref[h][:, 0:HEAD_DIM]


def _dsa_sample(layer, page_table, qh, iqh, ikw, knt, vnt, iknt, bias, cik_t, ck_t, cv_t):
    bsz = qh.shape[0]
    tp = SAMPLE_T_PAD
    any_spec = pl.BlockSpec(memory_space=pl.ANY)
    grid_spec = pltpu.PrefetchScalarGridSpec(
        num_scalar_prefetch=1,
        grid=(bsz,),
        in_specs=[pl.BlockSpec((1, N_HEADS, tp, HEAD_DIM), lambda b, pt: (b, 0, 0, 0)),
                  pl.BlockSpec((1, IDX_HEADS, tp, IDX_DIM), lambda b, pt: (b, 0, 0, 0)),
                  pl.BlockSpec((1, tp, LANES), lambda b, pt: (b, 0, 0)),
                  pl.BlockSpec((1, N_HEADS, HEAD_DIM, NEW_PAD), lambda b, pt: (b, 0, 0, 0)),
                  pl.BlockSpec((1, N_HEADS, HEAD_DIM, NEW_PAD), lambda b, pt: (b, 0, 0, 0)),
                  pl.BlockSpec((1, IDX_DIM, NEW_PAD), lambda b, pt: (b, 0, 0)),
                  pl.BlockSpec(bias.shape, lambda b, pt: (0, 0, 0, 0)),
                  any_spec, any_spec, any_spec],
        out_specs=pl.BlockSpec((1, tp, ATTN_W), lambda b, pt: (b, 0, 0)),
        scratch_shapes=[pltpu.VMEM((IDX_DIM, PAST_LEN), f32),
                        pltpu.VMEM((N_KV_GROUPS, N_HEADS, HEAD_DIM, KV_GROUP), f32),
                        pltpu.VMEM((N_KV_GROUPS, N_HEADS, HEAD_DIM, KV_GROUP), f32),
                        pltpu.VMEM((tp, S_KEYS), f32),
                        pltpu.VMEM((N_HEADS, tp, LANES), f32),
                        pltpu.VMEM((N_HEADS, tp, LANES), f32),
                        pltpu.VMEM((N_HEADS, tp, HEAD_DIM), f32),
                        pltpu.SemaphoreType.DMA((1,)),
                        pltpu.SemaphoreType.DMA((N_KV_GROUPS,))],
    )
    return pl.pallas_call(
        functools.partial(_dsa_sample_kernel, layer),
        grid_spec=grid_spec,
        out_shape=jax.ShapeDtypeStruct((bsz, tp, ATTN_W), f32),
        compiler_params=_params(("arbitrary",)),
        name="dsa_sample",
    )(page_table, qh, iqh, ikw, knt, vnt, iknt, bias, cik_t, ck_t, cv_t)


def _prompt_buckets():
    k = np.arange(TQ)[:, None]
    q = np.arange(TQ)[None, :]
    return np.stack([_rel_bucket_np(q - k), _rel_bucket_np(TQ + q - k)])


def _sample_buckets():
    t = np.arange(SAMPLE_T_PAD)[:, None]
    k = np.arange(KV_GROUP)[None, :]
    last = _rel_bucket_np(PAST_LEN + t - ((N_KV_GROUPS - 1) * KV_GROUP + k))
    new = np.full((SAMPLE_T_PAD, KV_GROUP), N_BUCKETS - 1, np.int32)
    new[:, :NEW_PAD] = _rel_bucket_np(t - np.arange(NEW_PAD)[None, :])
    return np.stack([last, new])


def _heads_major(x, bsz, t, heads, dim):
    return x.reshape(bsz, t, heads, dim).transpose(0, 2, 1, 3)


def kernel(x_prompt, x_sample, mem_prompt, cache_k, cache_v, cache_idx_k, state_conv, cache_mem_k, cache_mem_v, page_table, ln_g, ln_b, w_ff_gate, w_ff_up, w_ff_down, w_in, conv_dw, conv_db, conv_ln_g, conv_ln_b, w_mem_kv, w_br_attn, w_br_conv, w_br_mem, w_o, rel_bias):
    bp, sp, _ = x_prompt.shape
    bs, ts, _ = x_sample.shape
    depth = w_in.shape[0]

    wg = w_ff_gate.astype(bf16).reshape(depth, 2, D_MODEL, N_FF_CHUNKS, FF_CHUNK).transpose(0, 1, 3, 2, 4)
    wu = w_ff_up.astype(bf16).reshape(depth, 2, D_MODEL, N_FF_CHUNKS, FF_CHUNK).transpose(0, 1, 3, 2, 4)
    wgu = jnp.concatenate([wg, wu], axis=-1)
    wd = w_ff_down.astype(bf16).reshape(depth, 2, N_FF_CHUNKS, FF_CHUNK, D_MODEL)
    w_inb = w_in.astype(bf16)
    edges = np.cumsum([0, ATTN_W, ATTN_W, ATTN_W, IDX_HEADS * IDX_DIM, IDX_DIM + IDX_HEADS,
                       CONV_CH, CONV_CH, MEM_HEADS * MEM_HEAD_DIM, D_MODEL, D_MODEL, D_MODEL])

    nat = lambda *dts: (False, dts)
    fm = lambda *dts: (True, dts)

    def project(x, l, seq, tm, prompt):
        wq, wk, wv, wiq, wikw, wca, wcb_, wmq, wg0, wg1, wg2 = (
            w_inb[l, :, edges[i]:edges[i + 1]] for i in range(len(edges) - 1))
        w_ik = wikw[:, :IDX_DIM]
        wikw = jnp.pad(wikw, ((0, 0), (0, LANES - wikw.shape[1])))
        names = ["q", "iqb", "ikw", "cua", "cub", "mq"]
        ws = [wq, wiq, wikw, wca, wcb_, wmq]
        plan = [nat(f32), nat(bf16), nat(f32)] + [nat(f32)] * 3
        if prompt:
            names += ["kb", "kt", "vt", "vtb", "ik2"]
            ws += [wk, wk.T, wv.T, jnp.concatenate([w_ik, w_ik], axis=1)]
            plan += [nat(bf16), fm(f32), fm(f32, bf16), nat(bf16)]
        else:
            names += ["k", "v"]
            ws += [wk, wv]
            plan += [nat(f32), nat(f32)]
        return dict(zip(names, _proj(x, ws, tuple(plan), tm, seq)))

    w_memb = w_mem_kv.astype(bf16)
    wab, wcb, wmb, wob = (w.astype(bf16) for w in (w_br_attn, w_br_conv, w_br_mem, w_o))
    lng = ln_g.reshape(depth, 3, 1, D_MODEL)
    lnb = ln_b.reshape(depth, 3, 1, D_MODEL)

    bias_p = _bias_tiles(rel_bias, jnp.asarray(_prompt_buckets()))
    bias_s = _bias_tiles(rel_bias, jnp.asarray(_sample_buckets()))

    ck_t = cache_k.transpose(0, 1, 3, 4, 2)
    cv_t = cache_v.transpose(0, 1, 3, 4, 2)
    cik_t = cache_idx_k.transpose(0, 1, 3, 2)

    def layer(x, l, bsz, t, tm, prompt, mix):
        x = _ffn_ln(x, wgu[l, 0], wd[l, 0], lng[l, 0], lnb[l, 0], tm)
        p = project(x, l, t if prompt else x.shape[0], min(tm, 512), prompt)
        a, c, m, state = mix(p)
        w_gates = jnp.stack([w_inb[l, :, edges[i]:edges[i + 1]] for i in (8, 9, 10)])
        x = _merge_ln(x, a, c, m, w_gates, wab[l], wcb[l], wmb[l], wob[l],
                      lng[l, 1], lnb[l, 1], min(tm, 512))
        x = _ffn_ln(x, wgu[l, 1], wd[l, 1], lng[l, 2], lnb[l, 2], tm)
        return x, state

    def conv_args(l):
        return (conv_dw[l], conv_db[l].reshape(1, CONV_CH), conv_ln_g[l].reshape(1, CONV_CH),
                conv_ln_b[l].reshape(1, CONV_CH))

    def prompt_mix(l):
        def mix(p):
            ikw = p["ikw"]
            a = _dsa_prompt(p["q"].reshape(bp, sp, ATTN_W), p["iqb"].reshape(bp, sp, IDX_HEADS * IDX_DIM),
                            ikw.reshape(bp, sp, LANES), p["kb"].reshape(bp, sp, ATTN_W), p["vtb"],
                            p["ik2"].reshape(bp, sp, LANES), bias_p)
            prev = jnp.zeros((bp, STATE_ROWS, CONV_CH), f32)
            c, cst = _conv_module(p["cua"].reshape(bp, sp, CONV_CH), p["cub"].reshape(bp, sp, CONV_CH),
                                  prev, *conv_args(l), 512)
            mem = mem_prompt.reshape(bp * MEM_LEN, D_MODEL)
            mk, mv = _proj(mem, [w_memb[l, :, :ATTN_W], w_memb[l, :, ATTN_W:]], (nat(f32), nat(f32)),
                           256, bp * MEM_LEN)
            m = _mem_attn(p["mq"].reshape(bp, sp, ATTN_W), mk.reshape(bp, MEM_LEN, ATTN_W),
                          mv.reshape(bp, MEM_LEN, ATTN_W), 512)
            state = (p["kt"], p["vt"], ikw[:, :IDX_DIM].reshape(bp, sp, IDX_DIM), cst[:, STATE_PAD:],
                     mk.reshape(bp, MEM_LEN, MEM_HEADS, MEM_HEAD_DIM),
                     mv.reshape(bp, MEM_LEN, MEM_HEADS, MEM_HEAD_DIM))
            return (a.reshape(bp * sp, ATTN_W), c.reshape(bp * sp, CONV_CH),
                    m.reshape(bp * sp, ATTN_W), state)
        return mix

    def pad_to(x, axis, size):
        pad = [(0, 0)] * x.ndim
        pad[axis] = (0, size - x.shape[axis])
        return jnp.pad(x, pad)

    def sample_mix(l):
        def mix(p):
            q, k, v, iqb, ikw, cua, cub, mq = (p[n] for n in ("q", "k", "v", "iqb", "ikw", "cua", "cub", "mq"))
            tp = SAMPLE_T_PAD
            qh = pad_to(_heads_major(q, bs, ts, N_HEADS, HEAD_DIM), 2, tp)
            iqh = pad_to(_heads_major(iqb, bs, ts, IDX_HEADS, IDX_DIM), 2, tp)
            knt = pad_to(k.reshape(bs, ts, N_HEADS, HEAD_DIM).transpose(0, 2, 3, 1), 3, NEW_PAD).astype(bf16)
            vnt = pad_to(v.reshape(bs, ts, N_HEADS, HEAD_DIM).transpose(0, 2, 3, 1), 3, NEW_PAD).astype(bf16)
            iknt = pad_to(ikw[:, :IDX_DIM].reshape(bs, ts, IDX_DIM).transpose(0, 2, 1), 2, NEW_PAD).astype(bf16)
            ikw3 = pad_to(ikw.reshape(bs, ts, LANES), 1, tp)
            a = _dsa_sample(l, page_table, qh, iqh, ikw3, knt, vnt, iknt, bias_s,
                            cik_t, ck_t, cv_t)[:, :ts]
            prev = jnp.pad(state_conv[l], ((0, 0), (STATE_PAD, 0), (0, 0)))
            c, cst = _conv_module(cua.reshape(bs, ts, CONV_CH), cub.reshape(bs, ts, CONV_CH), prev,
                                  *conv_args(l), ts)
            mqp = pad_to(mq.reshape(bs, ts, ATTN_W), 1, tp)
            m = _mem_attn(mqp, cache_mem_k[l].reshape(bs, MEM_LEN, ATTN_W),
                          cache_mem_v[l].reshape(bs, MEM_LEN, ATTN_W), tp)[:, :ts]
            state = (k.reshape(bs, ts, N_HEADS, HEAD_DIM), v.reshape(bs, ts, N_HEADS, HEAD_DIM),
                     ikw[:, :IDX_DIM].reshape(bs, ts, IDX_DIM), cst[:, STATE_PAD:])
            return (a.reshape(bs * ts, ATTN_W), c.reshape(bs * ts, CONV_CH),
                    m.reshape(bs * ts, ATTN_W), state)
        return mix

    xp = x_prompt.reshape(bp * sp, D_MODEL)
    xs = x_sample.reshape(bs * ts, D_MODEL)
    st_p, st_s = [], []
    for l in range(depth):
        xp, s_p = layer(xp, l, bp, sp, 1024, True, prompt_mix(l))
        xs, s_s = layer(xs, l, bs, ts, bs * ts, False, sample_mix(l))
        st_p.append(s_p)
        st_s.append(s_s)

    outs_p = [jnp.stack([s[i] for s in st_p]) for i in range(6)]
    for i in (0, 1):
        outs_p[i] = outs_p[i].reshape(depth, bp, N_HEADS, HEAD_DIM, sp).transpose(0, 1, 4, 2, 3)
    outs_s = [jnp.stack([s[i] for s in st_s]) for i in range(4)]
    return (xp.reshape(bp, sp, D_MODEL), xs.reshape(bs, ts, D_MODEL), *outs_p, *outs_s)
```
